```python
import math
import jax, jax.numpy as jnp
from jax import lax
import numpy as np

D_MODEL = 1024
BATCH = 32
SEQ = 2048
DEPTH = 1

EPS = 1e-6
A_WIDTH = 768
A_GROUPS = 4
A_GROUP_DIM = A_WIDTH // A_GROUPS
CHUNK = 128
B_PATTERNS = ((128, 1), (512, 4), (2048, 16))
B_GROUPS = len(B_PATTERNS)
B_HEADS_PER_GROUP = 4
B_HEADS = B_GROUPS * B_HEADS_PER_GROUP
B_HEAD_DIM = 64
B_QKV_WIDTH = B_HEADS * B_HEAD_DIM
B_OUT_WIDTH = B_HEADS_PER_GROUP * B_HEAD_DIM
BLOCK = 128
MEM_LEN = 256
M_HEADS = 4
M_HEAD_DIM = 128
M_WIDTH = M_HEADS * M_HEAD_DIM
N_BRANCHES = 3
REL_BUCKETS = 32
REL_MAX_DISTANCE = 2048
IN_SIZES = (A_WIDTH, A_WIDTH, A_WIDTH,
            B_QKV_WIDTH, B_QKV_WIDTH, B_QKV_WIDTH, B_OUT_WIDTH,
            M_WIDTH, M_WIDTH,
            N_BRANCHES * D_MODEL)
IN_TOTAL = sum(IN_SIZES)

kernel_name = "hybrid_sgu_dilated_memory_block"


def rms_norm(x, w):
    xf = x.astype(jnp.float32)
    y = xf * lax.rsqrt(jnp.mean(xf * xf, axis=-1, keepdims=True) + EPS)
    return (y * w.astype(jnp.float32)).astype(x.dtype)


def layer_norm(x, w, b):
    xf = x.astype(jnp.float32)
    mu = jnp.mean(xf, axis=-1, keepdims=True)
    xc = xf - mu
    y = xc * lax.rsqrt(jnp.mean(xc * xc, axis=-1, keepdims=True) + EPS)
    return (y * w.astype(jnp.float32) + b.astype(jnp.float32)).astype(x.dtype)


def t5_causal_bucket(dist):
    max_exact = REL_BUCKETS // 2
    is_small = dist < max_exact
    df = jnp.maximum(dist, 1).astype(jnp.float32)
    large = max_exact + (jnp.log(df / max_exact) / math.log(REL_MAX_DISTANCE / max_exact)
                         * (REL_BUCKETS - max_exact)).astype(jnp.int32)
    large = jnp.minimum(large, REL_BUCKETS - 1)
    return jnp.where(is_small, dist, large)


def chunked_spatial_gating(u, v, w_s, b_s):
    bn, s, _ = u.shape
    nc = s // CHUNK
    causal = jnp.tril(jnp.ones((CHUNK, CHUNK), dtype=bool))
    w = jnp.where(causal[None], w_s, 0).astype(v.dtype)
    vc = v.reshape(bn, nc, CHUNK, A_GROUPS, A_GROUP_DIM)
    mixed = jnp.einsum('gts,bcsgd->bctgd', w, vc) + b_s.T.astype(v.dtype)[None, None, :, :, None]
    return u * mixed.reshape(bn, s, A_WIDTH)


def dilated_window_attention(q, k, v, bias_table, dilation, win_steps):
    bn, s, h, hd = q.shape
    L = s // dilation
    bd = bn * dilation

    def to_residue(t):
        return jnp.moveaxis(t.reshape((bn, L, dilation) + t.shape[2:]), 2, 1).reshape((bd, L) + t.shape[2:])

    def from_residue(t):
        t = t.reshape((bn, dilation, L) + t.shape[2:])
        return jnp.moveaxis(t, 1, 2).reshape((bn, s) + t.shape[3:])

    qs, ks, vs = to_residue(q), to_residue(k), to_residue(v)
    nb = -(-L // BLOCK)
    lp = nb * BLOCK
    pad = lp - L
    qb = jnp.pad(qs, ((0, 0), (0, pad), (0, 0), (0, 0))).reshape(bd, nb, BLOCK, h, hd)

    def kv_blocks(t):
        tp = jnp.pad(t, ((0, 0), (BLOCK, pad), (0, 0), (0, 0))).reshape(bd, nb + 1, BLOCK, h, hd)
        return jnp.concatenate([tp[:, :-1], tp[:, 1:]], axis=2)

    kb, vb = kv_blocks(ks), kv_blocks(vs)
    qi = jnp.arange(BLOCK)[:, None]
    kj = jnp.arange(2 * BLOCK)[None, :]
    step = qi + BLOCK - kj
    in_window = (step >= 0) & (step <= win_steps)
    key_pos = jnp.arange(nb)[:, None, None] * BLOCK - BLOCK + kj[None]
    valid = in_window[None] & (key_pos >= 0)
    bucket = t5_causal_bucket(jnp.maximum(step, 0) * dilation)
    bias = jnp.transpose(bias_table[bucket], (2, 0, 1)).astype(jnp.float32)

    scores = jnp.einsum('bnqhd,bnkhd->bnhqk', qb.astype(jnp.float32), kb.astype(jnp.float32)) * (hd ** -0.5)
    scores = jnp.where(valid[None, :, None], scores + bias[None, None], -1e30)
    m = jnp.max(scores, axis=-1, keepdims=True)
    p = jnp.exp(scores - m)
    l = jnp.sum(p, axis=-1)
    o = jnp.einsum('bnhqk,bnkhd->bnqhd', p, vb.astype(jnp.float32))
    o = o / jnp.transpose(l, (0, 1, 3, 2))[..., None]
    lse = jnp.transpose(m[..., 0] + jnp.log(l), (0, 1, 3, 2))
    o = o.reshape(bd, lp, h, hd)[:, :L]
    lse = lse.reshape(bd, lp, h)[:, :L]
    return from_residue(o), from_residue(lse)


def setup_inputs(seed: int = 0) -> dict:
    key = jax.random.key(seed)
    ks = jax.random.split(key, 22)
    f32 = jnp.float32
    n = lambda k, shape: jax.random.normal(k, shape, f32)
    return {
        "x": n(ks[0], (BATCH, SEQ, D_MODEL)),
        "mem": n(ks[1], (BATCH, MEM_LEN, D_MODEL)),
        "norm_w": 1.0 + 0.02 * n(ks[2], (D_MODEL,)),
        "w_in": n(ks[3], (D_MODEL, IN_TOTAL)) * D_MODEL ** -0.5,
        "gate_b": 0.01 * n(ks[4], (N_BRANCHES, D_MODEL)),
        "a_v_norm_w": 1.0 + 0.02 * n(ks[5], (A_WIDTH,)),
        "a_v_norm_b": 0.02 * n(ks[6], (A_WIDTH,)),
        "a_spatial_w": n(ks[7], (A_GROUPS, CHUNK, CHUNK)) * CHUNK ** -0.5,
        "a_spatial_b": 1.0 + 0.02 * n(ks[8], (A_GROUPS, CHUNK)),
        "b_q_norm_w": 1.0 + 0.02 * n(ks[9], (B_HEAD_DIM,)),
        "b_k_norm_w": 1.0 + 0.02 * n(ks[10], (B_HEAD_DIM,)),
        "rel_bias": 0.2 * n(ks[11], (REL_BUCKETS, B_HEADS)),
        "mem_norm_w": 1.0 + 0.02 * n(ks[12], (D_MODEL,)),
        "m_w_kv": n(ks[13], (D_MODEL, 2 * M_WIDTH)) * D_MODEL ** -0.5,
        "m_q_norm_w": 1.0 + 0.02 * n(ks[14], (M_HEAD_DIM,)),
        "m_k_norm_w": 1.0 + 0.02 * n(ks[15], (M_HEAD_DIM,)),
        "proj_a": n(ks[16], (A_WIDTH, D_MODEL)) * A_WIDTH ** -0.5,
        "proj_b": n(ks[17], (B_OUT_WIDTH, D_MODEL)) * B_OUT_WIDTH ** -0.5,
        "proj_m": n(ks[18], (M_WIDTH, D_MODEL)) * M_WIDTH ** -0.5,
        "w_out": n(ks[19], (D_MODEL, D_MODEL)) * D_MODEL ** -0.5,
    }


def reference(x, mem, norm_w, w_in, gate_b, a_v_norm_w, a_v_norm_b, a_spatial_w, a_spatial_b,
              b_q_norm_w, b_k_norm_w, rel_bias, mem_norm_w, m_w_kv, m_q_norm_w, m_k_norm_w,
              proj_a, proj_b, proj_m, w_out):
    bn, s, _ = x.shape
    split_at = np.cumsum(IN_SIZES)[:-1].tolist()
    for _layer in range(DEPTH):
        h = rms_norm(x, norm_w)
        proj = h @ w_in
        a_u, a_v, a_z, b_q, b_k, b_v, b_z, m_q, m_z, g = jnp.split(proj, split_at, axis=-1)

        u = jax.nn.gelu(a_u, approximate=False)
        vv = layer_norm(jax.nn.gelu(a_v, approximate=False), a_v_norm_w, a_v_norm_b)
        y_a = chunked_spatial_gating(u, vv, a_spatial_w, a_spatial_b) * jax.nn.silu(a_z)

        q = rms_norm(b_q.reshape(bn, s, B_HEADS, B_HEAD_DIM), b_q_norm_w)
        k = rms_norm(b_k.reshape(bn, s, B_HEADS, B_HEAD_DIM), b_k_norm_w)
        v = b_v.reshape(bn, s, B_HEADS, B_HEAD_DIM)
        outs, lses = [], []
        for gi, (window, dilation) in enumerate(B_PATTERNS):
            hs = slice(gi * B_HEADS_PER_GROUP, (gi + 1) * B_HEADS_PER_GROUP)
            o_g, lse_g = dilated_window_attention(q[:, :, hs], k[:, :, hs], v[:, :, hs],
                                                  rel_bias[:, hs], dilation, window // dilation)
            outs.append(o_g)
            lses.append(lse_g)
        o_all = jnp.stack(outs, axis=0)
        wts = jax.nn.softmax(jnp.stack(lses, axis=0), axis=0)
        y_b = jnp.sum(wts[..., None] * o_all, axis=0).reshape(bn, s, B_OUT_WIDTH).astype(x.dtype)
        y_b = y_b * jax.nn.silu(b_z)

        kv = rms_norm(mem, mem_norm_w) @ m_w_kv
        mk, mv = jnp.split(kv, 2, axis=-1)
        mq = rms_norm(m_q.reshape(bn, s, M_HEADS, M_HEAD_DIM), m_q_norm_w)
        mk = rms_norm(mk.reshape(bn, MEM_LEN, M_HEADS, M_HEAD_DIM), m_k_norm_w)
        mv = mv.reshape(bn, MEM_LEN, M_HEADS, M_HEAD_DIM)
        sc = jnp.einsum('bshd,bmhd->bhsm', mq.astype(jnp.float32), mk.astype(jnp.float32)) * (M_HEAD_DIM ** -0.5)
        pm = jax.nn.softmax(sc, axis=-1)
        y_m = jnp.einsum('bhsm,bmhd->bshd', pm, mv.astype(jnp.float32)).reshape(bn, s, M_WIDTH).astype(x.dtype)
        y_m = y_m * jax.nn.silu(m_z)

        gates = jax.nn.sigmoid((g.reshape(bn, s, N_BRANCHES, D_MODEL) + gate_b).astype(jnp.float32)).astype(x.dtype)
        merged = (gates[:, :, 0] * (y_a @ proj_a)
                  + gates[:, :, 1] * (y_b @ proj_b)
                  + gates[:, :, 2] * (y_m @ proj_m))
        x = x + merged @ w_out
    return x
```

```python
import functools
import math

import jax
import jax.numpy as jnp
from jax import lax
from jax.experimental import pallas as pl
from jax.experimental.pallas import tpu as pltpu

EPS = 1e-6
D_MODEL = 1024
A_WIDTH = 768
A_GROUPS = 4
A_GROUP_DIM = A_WIDTH // A_GROUPS
CHUNK = 128
B_PATTERNS = ((128, 1), (512, 4), (2048, 16))
B_GROUPS = 3
B_HEADS_PER_GROUP = 4
B_HEAD_DIM = 64
B_GROUP_WIDTH = B_HEADS_PER_GROUP * B_HEAD_DIM
B_QKV_WIDTH = B_GROUPS * B_GROUP_WIDTH
BLOCK = 128
MEM_LEN = 256
M_HEADS = 4
M_HEAD_DIM = 128
M_WIDTH = M_HEADS * M_HEAD_DIM
REL_BUCKETS = 32
REL_MAX_DISTANCE = 2048
NEG = -1e30

OFF_AU, OFF_AV, OFF_AZ = 0, 768, 1536
OFF_BQ, OFF_BK, OFF_BV = 2304, 3072, 3840
OFF_BZ, OFF_MQ, OFF_MZ, OFF_G = 4608, 4864, 5376, 5888
IN_TOTAL = 8960

VMEM_LIMIT_BYTES = 56 * 1024 * 1024

BF16 = jnp.bfloat16
F32 = jnp.float32


def _dot(a, b):
    return jnp.dot(a, b, preferred_element_type=F32)


def _dot_nt(a, b):
    return lax.dot_general(a, b, (((1,), (1,)), ((), ())), preferred_element_type=F32)


def _rms_rows(xf):
    return xf * lax.rsqrt(jnp.mean(xf * xf, axis=-1, keepdims=True) + EPS)


def _gelu_exact(t):
    return 0.5 * t * (1.0 + lax.erf(t * (2.0 ** -0.5)))


def _const_spec(shape):
    nd = len(shape)
    return pl.BlockSpec(shape, lambda *_: (0,) * nd, pipeline_mode=pl.Buffered(1))


def _t5_causal_bucket(dist):
    max_exact = REL_BUCKETS // 2
    is_small = dist < max_exact
    df = jnp.maximum(dist, 1).astype(F32)
    large = max_exact + (jnp.log(df / max_exact) / math.log(REL_MAX_DISTANCE / max_exact)
                         * (REL_BUCKETS - max_exact)).astype(jnp.int32)
    large = jnp.minimum(large, REL_BUCKETS - 1)
    return jnp.where(is_small, dist, large)


def _bias_kernel(bucket_ref, valid_ref, rb_ref, out_ref):
    g = pl.program_id(0)
    bk = bucket_ref[0]
    valid = valid_ref[...] != 0
    for h in range(B_HEADS_PER_GROUP):
        acc = jnp.zeros((BLOCK, 2 * BLOCK), F32)
        for b in range(REL_BUCKETS):
            acc = jnp.where(bk == b, rb_ref[b, g * B_HEADS_PER_GROUP + h], acc)
        out_ref[0, h * BLOCK:(h + 1) * BLOCK, :] = jnp.where(valid, acc, NEG)


def _expand_bias(rel_bias):
    qi = jnp.arange(BLOCK, dtype=jnp.int32)[:, None]
    kj = jnp.arange(2 * BLOCK, dtype=jnp.int32)[None, :]
    step = qi + BLOCK - kj
    buckets = jnp.stack([_t5_causal_bucket(jnp.maximum(step, 0) * d) for _, d in B_PATTERNS])
    valid = ((step >= 0) & (step <= BLOCK)).astype(jnp.int32)
    return pl.pallas_call(
        _bias_kernel,
        grid=(B_GROUPS,),
        in_specs=[
            pl.BlockSpec((1, BLOCK, 2 * BLOCK), lambda g: (g, 0, 0)),
            pl.BlockSpec((BLOCK, 2 * BLOCK), lambda g: (0, 0)),
            pl.BlockSpec(memory_space=pltpu.SMEM),
        ],
        out_specs=pl.BlockSpec((1, B_HEADS_PER_GROUP * BLOCK, 2 * BLOCK), lambda g: (g, 0, 0)),
        out_shape=jax.ShapeDtypeStruct((B_GROUPS, B_HEADS_PER_GROUP * BLOCK, 2 * BLOCK), F32),
        name="rel_bias_expand",
    )(buckets, valid, rel_bias)


def _qkv_kernel(x_ref, nw_ref, w_ref, wq_ref, wk_ref, q_ref, k_ref, v_ref):
    h = (_rms_rows(x_ref[...]) * nw_ref[...]).astype(BF16)
    r = lax.broadcasted_iota(jnp.int32, (B_GROUP_WIDTH, B_GROUP_WIDTH), 0) // B_HEAD_DIM
    c = lax.broadcasted_iota(jnp.int32, (B_GROUP_WIDTH, B_GROUP_WIDTH), 1) // B_HEAD_DIM
    seg = (r == c).astype(BF16)

    def head_norm(t, w):
        ms = _dot((t * t).astype(BF16), seg) * (1.0 / B_HEAD_DIM)
        return (t * lax.rsqrt(ms + EPS) * w).astype(BF16)

    for g in range(B_GROUPS):
        lo = g * B_GROUP_WIDTH
        q = _dot(h, w_ref[:, lo:lo + B_GROUP_WIDTH])
        q_ref[g] = head_norm(q, wq_ref[...])
        k = _dot(h, w_ref[:, B_QKV_WIDTH + lo:B_QKV_WIDTH + lo + B_GROUP_WIDTH])
        k_ref[g] = head_norm(k, wk_ref[...])
        v_ref[g] = _dot(h, w_ref[:, 2 * B_QKV_WIDTH + lo:2 * B_QKV_WIDTH + lo + B_GROUP_WIDTH]).astype(BF16)


def _qkv_proj(x2, norm_w, w_qkv, wq_t, wk_t, tm):
    n = x2.shape[0]
    out = jax.ShapeDtypeStruct((B_GROUPS, n, B_GROUP_WIDTH), BF16)
    ospec = pl.BlockSpec((B_GROUPS, tm, B_GROUP_WIDTH), lambda i: (0, i, 0))
    return pl.pallas_call(
        _qkv_kernel,
        grid=(n // tm,),
        in_specs=[
            pl.BlockSpec((tm, D_MODEL), lambda i: (i, 0)),
            _const_spec((1, D_MODEL)),
            _const_spec((D_MODEL, 3 * B_QKV_WIDTH)),
            _const_spec((1, B_GROUP_WIDTH)),
            _const_spec((1, B_GROUP_WIDTH)),
        ],
        out_specs=[ospec, ospec, ospec],
        out_shape=[out, out, out],
        compiler_params=pltpu.CompilerParams(
            dimension_semantics=("parallel",), vmem_limit_bytes=VMEM_LIMIT_BYTES),
        name="qkv_proj",
    )(x2, norm_w, w_qkv, wq_t, wk_t)


def _attn_kernel(q_ref, k_ref, v_ref, bias_ref, o_ref, lse_ref, *, dilation, nb):
    lane_head = lax.broadcasted_iota(jnp.int32, (1, B_GROUP_WIDTH), 1) // B_HEAD_DIM
    masks = [lane_head == h for h in range(B_HEADS_PER_GROUP)]

    def block(qb, kb, vb, bias):
        qs = jnp.concatenate([jnp.where(m, qb, jnp.zeros_like(qb)) for m in masks], axis=0)
        s = _dot_nt(qs, kb) + bias
        mx = jnp.max(s, axis=-1, keepdims=True)
        p = jnp.exp(s - mx)
        l = jnp.sum(p, axis=-1, keepdims=True)
        o = _dot(p.astype(BF16), vb) / l
        lse = mx + jnp.log(l)
        o_out = jnp.zeros((BLOCK, B_GROUP_WIDTH), F32)
        lse_out = jnp.zeros((BLOCK, B_GROUP_WIDTH), F32)
        for h, m in enumerate(masks):
            rows = slice(h * BLOCK, (h + 1) * BLOCK)
            o_out = jnp.where(m, o[rows], o_out)
            lse_out = jnp.where(m, lse[rows], lse_out)
        return o_out.astype(o_ref.dtype), lse_out

    for r in range(dilation):
        cols = slice(r * B_GROUP_WIDTH, (r + 1) * B_GROUP_WIDTH)
        o0, l0 = block(q_ref[0, 0, 0:BLOCK, cols], k_ref[0, 0, 0:BLOCK, cols],
                       v_ref[0, 0, 0:BLOCK, cols], bias_ref[0, :, BLOCK:2 * BLOCK])
        o_ref[0, 0:BLOCK, cols] = o0
        lse_ref[0, 0:BLOCK, cols] = l0

        if nb > 1:
            def body(n, carry, cols=cols):
                q0 = pl.multiple_of(n * BLOCK, BLOCK)
                k0 = pl.multiple_of((n - 1) * BLOCK, BLOCK)
                on, ln = block(q_ref[0, 0, pl.ds(q0, BLOCK), cols], k_ref[0, 0, pl.ds(k0, 2 * BLOCK), cols],
                               v_ref[0, 0, pl.ds(k0, 2 * BLOCK), cols], bias_ref[0])
                o_ref[0, pl.ds(q0, BLOCK), cols] = on
                lse_ref[0, pl.ds(q0, BLOCK), cols] = ln
                return carry

            lax.fori_loop(1, nb, body, 0)


def _dilated_attention(q, k, v, bias, g, bn, s):
    _, d = B_PATTERNS[g]
    L = s // d
    nb = L // BLOCK
    w = d * B_GROUP_WIDTH
    view = lambda t: t.reshape(B_GROUPS, bn, L, w)
    in_spec = pl.BlockSpec((1, 1, L, w), lambda b: (g, b, 0, 0))
    out_spec = pl.BlockSpec((1, L, w), lambda b: (b, 0, 0))
    o, lse = pl.pallas_call(
        functools.partial(_attn_kernel, dilation=d, nb=nb),
        grid=(bn,),
        in_specs=[in_spec, in_spec, in_spec,
                  pl.BlockSpec((1, B_HEADS_PER_GROUP * BLOCK, 2 * BLOCK), lambda b: (g, 0, 0))],
        out_specs=[out_spec, out_spec],
        out_shape=[jax.ShapeDtypeStruct((bn, L, w), BF16), jax.ShapeDtypeStruct((bn, L, w), F32)],
        compiler_params=pltpu.CompilerParams(
            dimension_semantics=("parallel",), vmem_limit_bytes=VMEM_LIMIT_BYTES),
        name=f"dilated_attn_g{g}",
    )(view(q), view(k), view(v), bias)
    return o.reshape(bn * s, B_GROUP_WIDTH), lse.reshape(bn * s, B_GROUP_WIDTH)


def _memkv_kernel(mem_ref, nw_ref, w_ref, kw_ref, mk_ref, mv_ref):
    hm = (_rms_rows(mem_ref[0]) * nw_ref[...]).astype(BF16)
    kv = _dot(hm, w_ref[...])
    for h in range(M_HEADS):
        cols = slice(h * M_HEAD_DIM, (h + 1) * M_HEAD_DIM)
        mk_ref[0, :, cols] = (_rms_rows(kv[:, cols]) * kw_ref[...]).astype(BF16)
    mv_ref[0] = kv[:, M_WIDTH:].astype(BF16)


def _mem_kv(mem, mem_norm_w, w_kv, k_norm_w):
    bn = mem.shape[0]
    out = jax.ShapeDtypeStruct((bn, MEM_LEN, M_WIDTH), BF16)
    ospec = pl.BlockSpec((1, MEM_LEN, M_WIDTH), lambda b: (b, 0, 0))
    return pl.pallas_call(
        _memkv_kernel,
        grid=(bn,),
        in_specs=[
            pl.BlockSpec((1, MEM_LEN, D_MODEL), lambda b: (b, 0, 0)),
            _const_spec((1, D_MODEL)),
            _const_spec((D_MODEL, 2 * M_WIDTH)),
            _const_spec((1, M_HEAD_DIM)),
        ],
        out_specs=[ospec, ospec],
        out_shape=[out, out],
        compiler_params=pltpu.CompilerParams(dimension_semantics=("parallel",)),
        name="mem_kv",
    )(mem, mem_norm_w, w_kv, k_norm_w)


MW_AU, MW_AV, MW_AZ = 0, 768, 1536
MW_BZ, MW_MQ, MW_MZ = 2304, 2560, 3072
MW_G = 3584
MW_TOTAL = MW_G + 3 * D_MODEL


def _main_kernel(x_ref, o0_ref, o1_ref, o2_ref, l0_ref, l1_ref, l2_ref, mk_ref, mv_ref,
                 nw_ref, w_ref, gb_ref, lnw_ref, lnb_ref, sw_ref, sb_ref, mqw_ref,
                 pa_ref, pb_ref, pm_ref, wo_ref, out_ref, *, tm):
    xf = x_ref[...]
    h = (_rms_rows(xf) * nw_ref[...]).astype(BF16)

    def proj(lo, width):
        return _dot(h, w_ref[:, lo:lo + width])

    def gate(i):
        return jax.nn.sigmoid(proj(MW_G + i * D_MODEL, D_MODEL) + gb_ref[i:i + 1, :])

    u = _gelu_exact(proj(MW_AU, A_WIDTH))
    gv = _gelu_exact(proj(MW_AV, A_WIDTH))
    mu = jnp.mean(gv, axis=-1, keepdims=True)
    gc = gv - mu
    vv = gc * lax.rsqrt(jnp.mean(gc * gc, axis=-1, keepdims=True) + EPS) * lnw_ref[...] + lnb_ref[...]
    vv = vv.astype(BF16)
    ti = lax.broadcasted_iota(jnp.int32, (CHUNK, CHUNK), 0)
    si = lax.broadcasted_iota(jnp.int32, (CHUNK, CHUNK), 1)
    causal = si <= ti
    lane_group = lax.broadcasted_iota(jnp.int32, (1, A_WIDTH), 1) // A_GROUP_DIM
    ws = [jnp.where(causal, sw_ref[g], 0.0).astype(BF16) for g in range(A_GROUPS)]
    mixed_chunks = []
    for c in range(tm // CHUNK):
        vc = vv[c * CHUNK:(c + 1) * CHUNK, :]
        mc = _dot(ws[0], vc)
        for g in range(1, A_GROUPS):
            mc = jnp.where(lane_group == g, _dot(ws[g], vc), mc)
        mixed_chunks.append(mc + sb_ref[...])
    mixed = jnp.concatenate(mixed_chunks, axis=0)
    y_a = (u * mixed * jax.nn.silu(proj(MW_AZ, A_WIDTH))).astype(BF16)
    acc = gate(0) * _dot(y_a, pa_ref[...])

    mq = proj(MW_MQ, M_WIDTH)
    ym_heads = []
    for hd in range(M_HEADS):
        cols = slice(hd * M_HEAD_DIM, (hd + 1) * M_HEAD_DIM)
        qn = (_rms_rows(mq[:, cols]) * mqw_ref[...]).astype(BF16)
        sc = _dot_nt(qn, mk_ref[0, :, cols]) * (M_HEAD_DIM ** -0.5)
        sc = sc - jnp.max(sc, axis=-1, keepdims=True)
        p = jnp.exp(sc)
        l = jnp.sum(p, axis=-1, keepdims=True)
        ym_heads.append(_dot(p.astype(BF16), mv_ref[0, :, cols]) / l)
    y_m = (jnp.concatenate(ym_heads, axis=-1) * jax.nn.silu(proj(MW_MZ, M_WIDTH))).astype(BF16)
    acc = acc + gate(2) * _dot(y_m, pm_ref[...])

    l0, l1, l2 = l0_ref[...], l1_ref[...], l2_ref[...]
    lm = jnp.maximum(jnp.maximum(l0, l1), l2)
    e0, e1, e2 = jnp.exp(l0 - lm), jnp.exp(l1 - lm), jnp.exp(l2 - lm)
    y_b = (e0 * o0_ref[...].astype(F32) + e1 * o1_ref[...].astype(F32) + e2 * o2_ref[...].astype(F32)) / (e0 + e1 + e2)
    y_b = (y_b * jax.nn.silu(proj(MW_BZ, B_GROUP_WIDTH))).astype(BF16)
    acc = acc + gate(1) * _dot(y_b, pb_ref[...])

    out_ref[...] = xf + _dot(acc.astype(BF16), wo_ref[...])


def _main_block(x2, o_list, lse_list, mk, mv, norm_w, w_main, gate_b, lnw, lnb, sw, sb_full, mqw,
                proj_a, proj_b, proj_m, w_out, tm, s):
    n = x2.shape[0]
    steps_per_batch = s // tm
    tile = lambda width: pl.BlockSpec((tm, width), lambda i: (i, 0))
    mem_spec = pl.BlockSpec((1, MEM_LEN, M_WIDTH), lambda i: (i // steps_per_batch, 0, 0))
    return pl.pallas_call(
        functools.partial(_main_kernel, tm=tm),
        grid=(n // tm,),
        in_specs=[tile(D_MODEL)] + [tile(B_GROUP_WIDTH)] * 6 + [mem_spec, mem_spec] + [
            _const_spec((1, D_MODEL)),
            _const_spec((D_MODEL, MW_TOTAL)),
            _const_spec((3, D_MODEL)),
            _const_spec((1, A_WIDTH)),
            _const_spec((1, A_WIDTH)),
            _const_spec((A_GROUPS, CHUNK, CHUNK)),
            _const_spec((CHUNK, A_WIDTH)),
            _const_spec((1, M_HEAD_DIM)),
            _const_spec((A_WIDTH, D_MODEL)),
            _const_spec((B_GROUP_WIDTH, D_MODEL)),
            _const_spec((M_WIDTH, D_MODEL)),
            _const_spec((D_MODEL, D_MODEL)),
        ],
        out_specs=tile(D_MODEL),
        out_shape=jax.ShapeDtypeStruct((n, D_MODEL), F32),
        compiler_params=pltpu.CompilerParams(
            dimension_semantics=("parallel",), vmem_limit_bytes=VMEM_LIMIT_BYTES),
        name="main_block",
    )(x2, *o_list, *lse_list, mk, mv, norm_w, w_main, gate_b, lnw, lnb, sw, sb_full, mqw,
      proj_a, proj_b, proj_m, w_out)


def kernel(x, mem, norm_w, w_in, gate_b, a_v_norm_w, a_v_norm_b, a_spatial_w, a_spatial_b,
           b_q_norm_w, b_k_norm_w, rel_bias, mem_norm_w, m_w_kv, m_q_norm_w, m_k_norm_w,
           proj_a, proj_b, proj_m, w_out):
    bn, s, _ = x.shape
    n = bn * s
    x2 = x.reshape(n, D_MODEL)
    row = lambda v: v.reshape(1, -1)

    w_qkv = w_in[:, OFF_BQ:OFF_BZ].astype(BF16)
    w_main = jnp.concatenate([w_in[:, :OFF_BQ], w_in[:, OFF_BZ:]], axis=1).astype(BF16)
    wq_t = row(jnp.tile(b_q_norm_w * (B_HEAD_DIM ** -0.5), B_HEADS_PER_GROUP))
    wk_t = row(jnp.tile(b_k_norm_w, B_HEADS_PER_GROUP))
    sb_full = jnp.repeat(a_spatial_b.T, A_GROUP_DIM, axis=1)

    bias = _expand_bias(rel_bias)
    q, k, v = _qkv_proj(x2, row(norm_w), w_qkv, wq_t, wk_t, tm=512)
    o_list, lse_list = [], []
    for g in range(B_GROUPS):
        o_g, lse_g = _dilated_attention(q, k, v, bias, g, bn, s)
        o_list.append(o_g)
        lse_list.append(lse_g)
    mk, mv = _mem_kv(mem, row(mem_norm_w), m_w_kv.astype(BF16), row(m_k_norm_w))
    out = _main_block(x2, o_list, lse_list, mk, mv, row(norm_w), w_main, gate_b,
                      row(a_v_norm_w), row(a_v_norm_b), a_spatial_w, sb_full, row(m_q_norm_w),
                      proj_a.astype(BF16), proj_b.astype(BF16), proj_m.astype(BF16), w_out.astype(BF16),
                      tm=256, s=s)
    return out.reshape(bn, s, D_MODEL)
```

```python
import functools
import math

import jax
import jax.numpy as jnp
from jax import lax
from jax.experimental import pallas as pl
from jax.experimental.pallas import tpu as pltpu

EPS = 1e-6
D_MODEL = 1024
A_WIDTH = 768
A_GROUPS = 4
A_GROUP_DIM = A_WIDTH // A_GROUPS
CHUNK = 128
B_PATTERNS = ((128, 1), (512, 4), (2048, 16))
B_DILATIONS = tuple(d for _, d in B_PATTERNS)
B_GROUPS = 3
B_HEADS_PER_GROUP = 4
B_HEAD_DIM = 64
B_GROUP_WIDTH = B_HEADS_PER_GROUP * B_HEAD_DIM
B_QKV_WIDTH = B_GROUPS * B_GROUP_WIDTH
BLOCK = 128
MEM_LEN = 256
M_HEADS = 4
M_HEAD_DIM = 128
M_WIDTH = M_HEADS * M_HEAD_DIM
REL_BUCKETS = 32
REL_MAX_DISTANCE = 2048
NEG = -1e30

OFF_BQ = 2304
OFF_BZ = 4608

LANES = 128
SLABS = B_GROUP_WIDTH // LANES
VMEM_LIMIT_BYTES = 56 * 1024 * 1024

QKV_TM = 512
MAIN_TM = 256
MERGE_ROWS = 256

BF16 = jnp.bfloat16
F32 = jnp.float32


def _dot(a, b):
    return jnp.dot(a, b, preferred_element_type=F32)


def _dot_nt(a, b):
    return lax.dot_general(a, b, (((1,), (1,)), ((), ())), preferred_element_type=F32)


def _rms_rows(xf):
    return xf * lax.rsqrt(jnp.mean(xf * xf, axis=-1, keepdims=True) + EPS)


def _gelu_exact(t):
    return 0.5 * t * (1.0 + lax.erf(t * (2.0 ** -0.5)))


def _const_spec(shape):
    nd = len(shape)
    return pl.BlockSpec(shape, lambda *_: (0,) * nd, pipeline_mode=pl.Buffered(1))


def _t5_causal_bucket(dist):
    max_exact = REL_BUCKETS // 2
    is_small = dist < max_exact
    df = jnp.maximum(dist, 1).astype(F32)
    large = max_exact + (jnp.log(df / max_exact) / math.log(REL_MAX_DISTANCE / max_exact)
                         * (REL_BUCKETS - max_exact)).astype(jnp.int32)
    large = jnp.minimum(large, REL_BUCKETS - 1)
    return jnp.where(is_small, dist, large)


def _bias_kernel(bucket_ref, valid_ref, rb_ref, out_ref):
    g = pl.program_id(0)
    bk = bucket_ref[0]
    valid = valid_ref[...] != 0
    for h in range(B_HEADS_PER_GROUP):
        acc = jnp.zeros((BLOCK, 2 * BLOCK), F32)
        for b in range(REL_BUCKETS):
            acc = jnp.where(bk == b, rb_ref[b, g * B_HEADS_PER_GROUP + h], acc)
        out_ref[0, h * BLOCK:(h + 1) * BLOCK, :] = jnp.where(valid, acc, NEG)


def _expand_bias(rel_bias):
    qi = jnp.arange(BLOCK, dtype=jnp.int32)[:, None]
    kj = jnp.arange(2 * BLOCK, dtype=jnp.int32)[None, :]
    step = qi + BLOCK - kj
    buckets = jnp.stack([_t5_causal_bucket(jnp.maximum(step, 0) * d) for d in B_DILATIONS])
    valid = ((step >= 0) & (step <= BLOCK)).astype(jnp.int32)
    return pl.pallas_call(
        _bias_kernel,
        grid=(B_GROUPS,),
        in_specs=[
            pl.BlockSpec((1, BLOCK, 2 * BLOCK), lambda g: (g, 0, 0)),
            pl.BlockSpec((BLOCK, 2 * BLOCK), lambda g: (0, 0)),
            pl.BlockSpec(memory_space=pltpu.SMEM),
        ],
        out_specs=pl.BlockSpec((1, B_HEADS_PER_GROUP * BLOCK, 2 * BLOCK), lambda g: (g, 0, 0)),
        out_shape=jax.ShapeDtypeStruct((B_GROUPS, B_HEADS_PER_GROUP * BLOCK, 2 * BLOCK), F32),
        name="rel_bias_expand",
    )(buckets, valid, rel_bias)


def _qkv_kernel(x_ref, nw_ref, w_ref, wq_ref, wk_ref, *refs, tm):
    out_refs = refs[:3 * B_GROUPS]
    scratch = refs[3 * B_GROUPS:]
    h = (_rms_rows(x_ref[0]) * nw_ref[...]).astype(BF16)
    r = lax.broadcasted_iota(jnp.int32, (B_GROUP_WIDTH, B_GROUP_WIDTH), 0) // B_HEAD_DIM
    c = lax.broadcasted_iota(jnp.int32, (B_GROUP_WIDTH, B_GROUP_WIDTH), 1) // B_HEAD_DIM
    seg = (r == c).astype(BF16)

    def head_norm(t, w):
        ms = _dot((t * t).astype(BF16), seg) * (1.0 / B_HEAD_DIM)
        return t * lax.rsqrt(ms + EPS) * w

    def emit(g, which, t):
        out = out_refs[3 * g + which]
        d = B_DILATIONS[g]
        if d == 1:
            out[0] = t.astype(BF16)
            return
        scr = scratch[3 * (g - 1) + which]
        for j in range(SLABS):
            scr[j] = t[:, j * LANES:(j + 1) * LANES]
        for res in range(d):
            for j in range(SLABS):
                out[0, res, :, j * LANES:(j + 1) * LANES] = (
                    scr[j, pl.ds(res, tm // d, stride=d), :].astype(BF16))

    for g in range(B_GROUPS):
        lo = g * B_GROUP_WIDTH
        emit(g, 0, head_norm(_dot(h, w_ref[:, lo:lo + B_GROUP_WIDTH]), wq_ref[...]))
        lo += B_QKV_WIDTH
        emit(g, 1, head_norm(_dot(h, w_ref[:, lo:lo + B_GROUP_WIDTH]), wk_ref[...]))
        lo += B_QKV_WIDTH
        emit(g, 2, _dot(h, w_ref[:, lo:lo + B_GROUP_WIDTH]))


def _qkv_proj(x, norm_w, w_qkv, wq_t, wk_t):
    bn, s, _ = x.shape
    tm = QKV_TM
    out_shapes, out_specs, scratch = [], [], []
    for d in B_DILATIONS:
        if d == 1:
            shape = jax.ShapeDtypeStruct((bn, s, B_GROUP_WIDTH), BF16)
            spec = pl.BlockSpec((1, tm, B_GROUP_WIDTH), lambda b, j: (b, j, 0))
        else:
            shape = jax.ShapeDtypeStruct((bn, d, s // d, B_GROUP_WIDTH), BF16)
            spec = pl.BlockSpec((1, d, tm // d, B_GROUP_WIDTH), lambda b, j: (b, 0, j, 0))
            scratch += [pltpu.VMEM((SLABS, tm, LANES), F32)] * 3
        out_shapes += [shape] * 3
        out_specs += [spec] * 3
    return pl.pallas_call(
        functools.partial(_qkv_kernel, tm=tm),
        grid=(bn, s // tm),
        in_specs=[
            pl.BlockSpec((1, tm, D_MODEL), lambda b, j: (b, j, 0)),
            _const_spec((1, D_MODEL)),
            _const_spec((D_MODEL, 3 * B_QKV_WIDTH)),
            _const_spec((1, B_GROUP_WIDTH)),
            _const_spec((1, B_GROUP_WIDTH)),
        ],
        out_specs=out_specs,
        out_shape=out_shapes,
        scratch_shapes=scratch,
        compiler_params=pltpu.CompilerParams(
            dimension_semantics=("parallel", "parallel"), vmem_limit_bytes=VMEM_LIMIT_BYTES),
        name="qkv_proj",
    )(x, norm_w, w_qkv, wq_t, wk_t)


def _attn_kernel(q0, k0, v0, q1, k1, v1, q2, k2, v2, bias_ref, yb_ref, o_scr, lse_scr, *, s):
    lane_head = lax.broadcasted_iota(jnp.int32, (1, B_GROUP_WIDTH), 1) // B_HEAD_DIM
    masks = [lane_head == h for h in range(B_HEADS_PER_GROUP)]

    def block(qb, kb, vb, bias):
        qs = jnp.concatenate([jnp.where(m, qb, jnp.zeros_like(qb)) for m in masks], axis=0)
        sc = _dot_nt(qs, kb) + bias
        mx = jnp.max(sc, axis=-1, keepdims=True)
        p = jnp.exp(sc - mx)
        l = jnp.sum(p, axis=-1, keepdims=True)
        o = _dot(p.astype(BF16), vb) / l
        lse = mx + jnp.log(l)
        o_out = jnp.zeros((BLOCK, B_GROUP_WIDTH), F32)
        lse_out = jnp.zeros((BLOCK, B_GROUP_WIDTH), F32)
        for h, m in enumerate(masks):
            rows = slice(h * BLOCK, (h + 1) * BLOCK)
            o_out = jnp.where(m, o[rows], o_out)
            lse_out = jnp.where(m, lse[rows], lse_out)
        return o_out, lse_out

    def emit(g, start, res):
        d = B_DILATIONS[g]
        idx = pl.ds(start, BLOCK) if d == 1 else pl.ds(start, BLOCK, stride=d)
        o_out, lse_out = res
        for j in range(SLABS):
            o_scr[g, j, idx, :] = o_out[:, j * LANES:(j + 1) * LANES]
            lse_scr[g, j, idx, :] = lse_out[:, j * LANES:(j + 1) * LANES]

    def first_block(g, q, k, v):
        return block(q[0:BLOCK, :], k[0:BLOCK, :], v[0:BLOCK, :], bias_ref[g, :, BLOCK:2 * BLOCK])

    def later_block(g, q, k, v, n):
        q_lo = pl.multiple_of(n * BLOCK, BLOCK)
        k_lo = pl.multiple_of((n - 1) * BLOCK, BLOCK)
        return block(q[pl.ds(q_lo, BLOCK), :], k[pl.ds(k_lo, 2 * BLOCK), :],
                     v[pl.ds(k_lo, 2 * BLOCK), :], bias_ref[g])

    emit(0, 0, first_block(0, q0.at[0], k0.at[0], v0.at[0]))

    def g0_body(n, carry):
        emit(0, pl.multiple_of(n * BLOCK, BLOCK), later_block(0, q0.at[0], k0.at[0], v0.at[0], n))
        return carry

    lax.fori_loop(1, s // BLOCK, g0_body, 0)

    for g, (q, k, v) in ((1, (q1, k1, v1)), (2, (q2, k2, v2))):
        d = B_DILATIONS[g]
        nb = s // d // BLOCK

        def res_body(res, carry, g=g, q=q, k=k, v=v, d=d, nb=nb):
            qr, kr, vr = q.at[0, res], k.at[0, res], v.at[0, res]
            emit(g, res, first_block(g, qr, kr, vr))
            if nb > 1:
                def blk_body(n, c2):
                    emit(g, res + n * (BLOCK * d), later_block(g, qr, kr, vr, n))
                    return c2

                lax.fori_loop(1, nb, blk_body, 0)
            return carry

        lax.fori_loop(0, d, res_body, 0)

    def merge_body(c, carry):
        rows = pl.ds(pl.multiple_of(c * MERGE_ROWS, MERGE_ROWS), MERGE_ROWS)
        for j in range(SLABS):
            l0, l1, l2 = lse_scr[0, j, rows, :], lse_scr[1, j, rows, :], lse_scr[2, j, rows, :]
            lm = jnp.maximum(jnp.maximum(l0, l1), l2)
            e0, e1, e2 = jnp.exp(l0 - lm), jnp.exp(l1 - lm), jnp.exp(l2 - lm)
            y = (e0 * o_scr[0, j, rows, :] + e1 * o_scr[1, j, rows, :] + e2 * o_scr[2, j, rows, :]) / (e0 + e1 + e2)
            yb_ref[0, rows, j * LANES:(j + 1) * LANES] = y.astype(yb_ref.dtype)
        return carry

    lax.fori_loop(0, s // MERGE_ROWS, merge_body, 0)


def _dilated_attention(qkv, bias):
    bn, s, _ = qkv[0].shape
    in_specs = []
    for d in B_DILATIONS:
        if d == 1:
            spec = pl.BlockSpec((1, s, B_GROUP_WIDTH), lambda b: (b, 0, 0))
        else:
            spec = pl.BlockSpec((1, d, s // d, B_GROUP_WIDTH), lambda b: (b, 0, 0, 0))
        in_specs += [spec] * 3
    in_specs.append(_const_spec((B_GROUPS, B_HEADS_PER_GROUP * BLOCK, 2 * BLOCK)))
    return pl.pallas_call(
        functools.partial(_attn_kernel, s=s),
        grid=(bn,),
        in_specs=in_specs,
        out_specs=pl.BlockSpec((1, s, B_GROUP_WIDTH), lambda b: (b, 0, 0)),
        out_shape=jax.ShapeDtypeStruct((bn, s, B_GROUP_WIDTH), BF16),
        scratch_shapes=[pltpu.VMEM((B_GROUPS, SLABS, s, LANES), F32),
                        pltpu.VMEM((B_GROUPS, SLABS, s, LANES), F32)],
        compiler_params=pltpu.CompilerParams(
            dimension_semantics=("parallel",), vmem_limit_bytes=VMEM_LIMIT_BYTES),
        name="dilated_attn",
    )(*qkv, bias)


def _memkv_kernel(mem_ref, nw_ref, w_ref, kw_ref, mk_ref, mv_ref):
    hm = (_rms_rows(mem_ref[0]) * nw_ref[...]).astype(BF16)
    kv = _dot(hm, w_ref[...])
    for h in range(M_HEADS):
        cols = slice(h * M_HEAD_DIM, (h + 1) * M_HEAD_DIM)
        mk_ref[0, :, cols] = (_rms_rows(kv[:, cols]) * kw_ref[...]).astype(BF16)
    mv_ref[0] = kv[:, M_WIDTH:].astype(BF16)


def _mem_kv(mem, mem_norm_w, w_kv, k_norm_w):
    bn = mem.shape[0]
    out = jax.ShapeDtypeStruct((bn, MEM_LEN, M_WIDTH), BF16)
    ospec = pl.BlockSpec((1, MEM_LEN, M_WIDTH), lambda b: (b, 0, 0))
    return pl.pallas_call(
        _memkv_kernel,
        grid=(bn,),
        in_specs=[
            pl.BlockSpec((1, MEM_LEN, D_MODEL), lambda b: (b, 0, 0)),
            _const_spec((1, D_MODEL)),
            _const_spec((D_MODEL, 2 * M_WIDTH)),
            _const_spec((1, M_HEAD_DIM)),
        ],
        out_specs=[ospec, ospec],
        out_shape=[out, out],
        compiler_params=pltpu.CompilerParams(dimension_semantics=("parallel",)),
        name="mem_kv",
    )(mem, mem_norm_w, w_kv, k_norm_w)


MW_AU, MW_AV, MW_AZ = 0, 768, 1536
MW_BZ, MW_MQ, MW_MZ = 2304, 2560, 3072
MW_G = 3584
MW_TOTAL = MW_G + 3 * D_MODEL


def _main_kernel(x_ref, yb_ref, mk_ref, mv_ref,
                 nw_ref, w_ref, gb_ref, lnw_ref, lnb_ref, sw_ref, sb_ref, mqw_ref,
                 pa_ref, pb_ref, pm_ref, wo_ref, out_ref, *, tm):
    xf = x_ref[...]
    h = (_rms_rows(xf) * nw_ref[...]).astype(BF16)

    def proj(lo, width):
        return _dot(h, w_ref[:, lo:lo + width])

    def gate(i):
        return jax.nn.sigmoid(proj(MW_G + i * D_MODEL, D_MODEL) + gb_ref[i:i + 1, :])

    u = _gelu_exact(proj(MW_AU, A_WIDTH))
    gv = _gelu_exact(proj(MW_AV, A_WIDTH))
    mu = jnp.mean(gv, axis=-1, keepdims=True)
    gc = gv - mu
    vv = gc * lax.rsqrt(jnp.mean(gc * gc, axis=-1, keepdims=True) + EPS) * lnw_ref[...] + lnb_ref[...]
    vv = vv.astype(BF16)
    ti = lax.broadcasted_iota(jnp.int32, (CHUNK, CHUNK), 0)
    si = lax.broadcasted_iota(jnp.int32, (CHUNK, CHUNK), 1)
    causal = si <= ti
    lane_group = lax.broadcasted_iota(jnp.int32, (1, A_WIDTH), 1) // A_GROUP_DIM
    ws = [jnp.where(causal, sw_ref[g], 0.0).astype(BF16) for g in range(A_GROUPS)]
    mixed_chunks = []
    for c in range(tm // CHUNK):
        vc = vv[c * CHUNK:(c + 1) * CHUNK, :]
        mc = _dot(ws[0], vc)
        for g in range(1, A_GROUPS):
            mc = jnp.where(lane_group == g, _dot(ws[g], vc), mc)
        mixed_chunks.append(mc + sb_ref[...])
    mixed = jnp.concatenate(mixed_chunks, axis=0)
    y_a = (u * mixed * jax.nn.silu(proj(MW_AZ, A_WIDTH))).astype(BF16)
    acc = gate(0) * _dot(y_a, pa_ref[...])

    mq = proj(MW_MQ, M_WIDTH)
    ym_heads = []
    for hd in range(M_HEADS):
        cols = slice(hd * M_HEAD_DIM, (hd + 1) * M_HEAD_DIM)
        qn = (_rms_rows(mq[:, cols]) * mqw_ref[...]).astype(BF16)
        sc = _dot_nt(qn, mk_ref[0, :, cols]) * (M_HEAD_DIM ** -0.5)
        sc = sc - jnp.max(sc, axis=-1, keepdims=True)
        p = jnp.exp(sc)
        l = jnp.sum(p, axis=-1, keepdims=True)
        ym_heads.append(_dot(p.astype(BF16), mv_ref[0, :, cols]) / l)
    y_m = (jnp.concatenate(ym_heads, axis=-1) * jax.nn.silu(proj(MW_MZ, M_WIDTH))).astype(BF16)
    acc = acc + gate(2) * _dot(y_m, pm_ref[...])

    y_b = (yb_ref[...].astype(F32) * jax.nn.silu(proj(MW_BZ, B_GROUP_WIDTH))).astype(BF16)
    acc = acc + gate(1) * _dot(y_b, pb_ref[...])

    out_ref[...] = xf + _dot(acc.astype(BF16), wo_ref[...])


def _main_block(x2, yb, mk, mv, norm_w, w_main, gate_b, lnw, lnb, sw, sb_full, mqw,
                proj_a, proj_b, proj_m, w_out, s):
    n = x2.shape[0]
    tm = MAIN_TM
    steps_per_batch = s // tm
    tile = lambda width: pl.BlockSpec((tm, width), lambda i: (i, 0))
    mem_spec = pl.BlockSpec((1, MEM_LEN, M_WIDTH), lambda i: (i // steps_per_batch, 0, 0))
    return pl.pallas_call(
        functools.partial(_main_kernel, tm=tm),
        grid=(n // tm,),
        in_specs=[tile(D_MODEL), tile(B_GROUP_WIDTH), mem_spec, mem_spec,
                  _const_spec((1, D_MODEL)),
                  _const_spec((D_MODEL, MW_TOTAL)),
                  _const_spec((3, D_MODEL)),
                  _const_spec((1, A_WIDTH)),
                  _const_spec((1, A_WIDTH)),
                  _const_spec((A_GROUPS, CHUNK, CHUNK)),
                  _const_spec((CHUNK, A_WIDTH)),
                  _const_spec((1, M_HEAD_DIM)),
                  _const_spec((A_WIDTH, D_MODEL)),
                  _const_spec((B_GROUP_WIDTH, D_MODEL)),
                  _const_spec((M_WIDTH, D_MODEL)),
                  _const_spec((D_MODEL, D_MODEL))],
        out_specs=tile(D_MODEL),
        out_shape=jax.ShapeDtypeStruct((n, D_MODEL), F32),
        compiler_params=pltpu.CompilerParams(
            dimension_semantics=("parallel",), vmem_limit_bytes=VMEM_LIMIT_BYTES),
        name="main_block",
    )(x2, yb, mk, mv, norm_w, w_main, gate_b, lnw, lnb, sw, sb_full, mqw,
      proj_a, proj_b, proj_m, w_out)


def kernel(x, mem, norm_w, w_in, gate_b, a_v_norm_w, a_v_norm_b, a_spatial_w, a_spatial_b,
           b_q_norm_w, b_k_norm_w, rel_bias, mem_norm_w, m_w_kv, m_q_norm_w, m_k_norm_w,
           proj_a, proj_b, proj_m, w_out):
    bn, s, _ = x.shape
    n = bn * s
    row = lambda v: v.reshape(1, -1)

    w_qkv = w_in[:, OFF_BQ:OFF_BZ].astype(BF16)
    w_main = jnp.concatenate([w_in[:, :OFF_BQ], w_in[:, OFF_BZ:]], axis=1).astype(BF16)
    wq_t = row(jnp.tile(b_q_norm_w * (B_HEAD_DIM ** -0.5), B_HEADS_PER_GROUP))
    wk_t = row(jnp.tile(b_k_norm_w, B_HEADS_PER_GROUP))
    sb_full = jnp.repeat(a_spatial_b.T, A_GROUP_DIM, axis=1)

    bias = _expand_bias(rel_bias)
    qkv = _qkv_proj(x, row(norm_w), w_qkv, wq_t, wk_t)
    yb = _dilated_attention(qkv, bias)
    mk, mv = _mem_kv(mem, row(mem_norm_w), m_w_kv.astype(BF16), row(m_k_norm_w))
    out = _main_block(x.reshape(n, D_MODEL), yb.reshape(n, B_GROUP_WIDTH), mk, mv, row(norm_w), w_main, gate_b,
                      row(a_v_norm_w), row(a_v_norm_b), a_spatial_w, sb_full, row(m_q_norm_w),
                      proj_a.astype(BF16), proj_b.astype(BF16), proj_m.astype(BF16), w_out.astype(BF16), s)
    return out.reshape(bn, s, D_MODEL)
```

```python
import functools
import math

import jax
import jax.numpy as jnp
from jax import lax
from jax.experimental import pallas as pl
from jax.experimental.pallas import tpu as pltpu

EPS = 1e-6
D_MODEL = 1024
A_WIDTH = 768
A_GROUPS = 4
A_GROUP_DIM = A_WIDTH // A_GROUPS
CHUNK = 128
B_PATTERNS = ((128, 1), (512, 4), (2048, 16))
B_DILATIONS = tuple(d for _, d in B_PATTERNS)
B_GROUPS = 3
B_HEADS_PER_GROUP = 4
B_HEAD_DIM = 64
B_GROUP_WIDTH = B_HEADS_PER_GROUP * B_HEAD_DIM
B_QKV_WIDTH = B_GROUPS * B_GROUP_WIDTH
BLOCK = 128
MEM_LEN = 256
M_HEADS = 4
M_HEAD_DIM = 128
M_WIDTH = M_HEADS * M_HEAD_DIM
REL_BUCKETS = 32
REL_MAX_DISTANCE = 2048
NEG = -1e30

OFF_BQ = 2304
OFF_BZ = 4608

LANES = 128
SLABS = B_GROUP_WIDTH // LANES
VMEM_LIMIT_BYTES = 56 * 1024 * 1024

QKV_TM = 512
MAIN_TM = 512
MERGE_ROWS = 256

BF16 = jnp.bfloat16
F32 = jnp.float32


def _dot(a, b):
    return jnp.dot(a, b, preferred_element_type=F32)


def _dot_nt(a, b):
    return lax.dot_general(a, b, (((1,), (1,)), ((), ())), preferred_element_type=F32)


def _rms_rows(xf):
    return xf * lax.rsqrt(jnp.mean(xf * xf, axis=-1, keepdims=True) + EPS)


def _gelu_exact(t):
    return 0.5 * t * (1.0 + lax.erf(t * (2.0 ** -0.5)))


def _const_spec(shape):
    nd = len(shape)
    return pl.BlockSpec(shape, lambda *_: (0,) * nd, pipeline_mode=pl.Buffered(1))


def _t5_causal_bucket(dist):
    max_exact = REL_BUCKETS // 2
    is_small = dist < max_exact
    df = jnp.maximum(dist, 1).astype(F32)
    large = max_exact + (jnp.log(df / max_exact) / math.log(REL_MAX_DISTANCE / max_exact)
                         * (REL_BUCKETS - max_exact)).astype(jnp.int32)
    large = jnp.minimum(large, REL_BUCKETS - 1)
    return jnp.where(is_small, dist, large)


def _bias_kernel(bucket_ref, valid_ref, rb_ref, out_ref):
    g = pl.program_id(0)
    bk = bucket_ref[0]
    valid = valid_ref[...] != 0
    for h in range(B_HEADS_PER_GROUP):
        acc = jnp.zeros((BLOCK, 2 * BLOCK), F32)
        for b in range(REL_BUCKETS):
            acc = jnp.where(bk == b, rb_ref[b, g * B_HEADS_PER_GROUP + h], acc)
        out_ref[0, h * BLOCK:(h + 1) * BLOCK, :] = jnp.where(valid, acc, NEG)


def _expand_bias(rel_bias):
    qi = jnp.arange(BLOCK, dtype=jnp.int32)[:, None]
    kj = jnp.arange(2 * BLOCK, dtype=jnp.int32)[None, :]
    step = qi + BLOCK - kj
    buckets = jnp.stack([_t5_causal_bucket(jnp.maximum(step, 0) * d) for d in B_DILATIONS])
    valid = ((step >= 0) & (step <= BLOCK)).astype(jnp.int32)
    return pl.pallas_call(
        _bias_kernel,
        grid=(B_GROUPS,),
        in_specs=[
            pl.BlockSpec((1, BLOCK, 2 * BLOCK), lambda g: (g, 0, 0)),
            pl.BlockSpec((BLOCK, 2 * BLOCK), lambda g: (0, 0)),
            pl.BlockSpec(memory_space=pltpu.SMEM),
        ],
        out_specs=pl.BlockSpec((1, B_HEADS_PER_GROUP * BLOCK, 2 * BLOCK), lambda g: (g, 0, 0)),
        out_shape=jax.ShapeDtypeStruct((B_GROUPS, B_HEADS_PER_GROUP * BLOCK, 2 * BLOCK), F32),
        name="rel_bias_expand",
    )(buckets, valid, rel_bias)


def _qkv_kernel(x_ref, nw_ref, w_ref, wq_ref, wk_ref, *refs, tm):
    out_refs = refs[:3 * B_GROUPS]
    scratch = refs[3 * B_GROUPS:]
    h = (_rms_rows(x_ref[0]) * nw_ref[...]).astype(BF16)
    r = lax.broadcasted_iota(jnp.int32, (B_GROUP_WIDTH, B_GROUP_WIDTH), 0) // B_HEAD_DIM
    c = lax.broadcasted_iota(jnp.int32, (B_GROUP_WIDTH, B_GROUP_WIDTH), 1) // B_HEAD_DIM
    seg = (r == c).astype(BF16)

    def head_norm(t, w):
        ms = _dot((t * t).astype(BF16), seg) * (1.0 / B_HEAD_DIM)
        return t * lax.rsqrt(ms + EPS) * w

    def emit(g, which, t):
        out = out_refs[3 * g + which]
        d = B_DILATIONS[g]
        if d == 1:
            out[0] = t.astype(BF16)
            return
        scr = scratch[3 * (g - 1) + which]
        for j in range(SLABS):
            scr[j] = t[:, j * LANES:(j + 1) * LANES]
        for res in range(d):
            for j in range(SLABS):
                out[0, res, :, j * LANES:(j + 1) * LANES] = (
                    scr[j, pl.ds(res, tm // d, stride=d), :].astype(BF16))

    for g in range(B_GROUPS):
        lo = g * B_GROUP_WIDTH
        emit(g, 0, head_norm(_dot(h, w_ref[:, lo:lo + B_GROUP_WIDTH]), wq_ref[...]))
        lo += B_QKV_WIDTH
        emit(g, 1, head_norm(_dot(h, w_ref[:, lo:lo + B_GROUP_WIDTH]), wk_ref[...]))
        lo += B_QKV_WIDTH
        emit(g, 2, _dot(h, w_ref[:, lo:lo + B_GROUP_WIDTH]))


def _qkv_proj(x, norm_w, w_qkv, wq_t, wk_t):
    bn, s, _ = x.shape
    tm = QKV_TM
    out_shapes, out_specs, scratch = [], [], []
    for d in B_DILATIONS:
        if d == 1:
            shape = jax.ShapeDtypeStruct((bn, s, B_GROUP_WIDTH), BF16)
            spec = pl.BlockSpec((1, tm, B_GROUP_WIDTH), lambda b, j: (b, j, 0))
        else:
            shape = jax.ShapeDtypeStruct((bn, d, s // d, B_GROUP_WIDTH), BF16)
            spec = pl.BlockSpec((1, d, tm // d, B_GROUP_WIDTH), lambda b, j: (b, 0, j, 0))
            scratch += [pltpu.VMEM((SLABS, tm, LANES), F32)] * 3
        out_shapes += [shape] * 3
        out_specs += [spec] * 3
    return pl.pallas_call(
        functools.partial(_qkv_kernel, tm=tm),
        grid=(bn, s // tm),
        in_specs=[
            pl.BlockSpec((1, tm, D_MODEL), lambda b, j: (b, j, 0)),
            _const_spec((1, D_MODEL)),
            _const_spec((D_MODEL, 3 * B_QKV_WIDTH)),
            _const_spec((1, B_GROUP_WIDTH)),
            _const_spec((1, B_GROUP_WIDTH)),
        ],
        out_specs=out_specs,
        out_shape=out_shapes,
        scratch_shapes=scratch,
        compiler_params=pltpu.CompilerParams(
            dimension_semantics=("parallel", "parallel"), vmem_limit_bytes=VMEM_LIMIT_BYTES),
        name="qkv_proj",
    )(x, norm_w, w_qkv, wq_t, wk_t)


def _attn_kernel(q0, k0, v0, q1, k1, v1, q2, k2, v2, bias_ref, yb_ref, o_scr, lse_scr, *, s):
    lane_head = lax.broadcasted_iota(jnp.int32, (1, B_GROUP_WIDTH), 1) // B_HEAD_DIM
    masks = [lane_head == h for h in range(B_HEADS_PER_GROUP)]

    def block(qb, kb, vb, bias):
        qs = jnp.concatenate([jnp.where(m, qb, jnp.zeros_like(qb)) for m in masks], axis=0)
        sc = _dot_nt(qs, kb) + bias
        mx = jnp.max(sc, axis=-1, keepdims=True)
        p = jnp.exp(sc - mx)
        l = jnp.sum(p, axis=-1, keepdims=True)
        o = _dot(p.astype(BF16), vb) / l
        lse = mx + jnp.log(l)
        o_out = jnp.zeros((BLOCK, B_GROUP_WIDTH), F32)
        lse_out = jnp.zeros((BLOCK, B_GROUP_WIDTH), F32)
        for h, m in enumerate(masks):
            rows = slice(h * BLOCK, (h + 1) * BLOCK)
            o_out = jnp.where(m, o[rows], o_out)
            lse_out = jnp.where(m, lse[rows], lse_out)
        return o_out, lse_out

    def emit(g, start, res):
        d = B_DILATIONS[g]
        idx = pl.ds(start, BLOCK) if d == 1 else pl.ds(start, BLOCK, stride=d)
        o_out, lse_out = res
        for j in range(SLABS):
            o_scr[g, j, idx, :] = o_out[:, j * LANES:(j + 1) * LANES]
            lse_scr[g, j, idx, :] = lse_out[:, j * LANES:(j + 1) * LANES]

    def first_block(g, q, k, v):
        return block(q[0:BLOCK, :], k[0:BLOCK, :], v[0:BLOCK, :], bias_ref[g, :, BLOCK:2 * BLOCK])

    def later_block(g, q, k, v, n):
        q_lo = pl.multiple_of(n * BLOCK, BLOCK)
        k_lo = pl.multiple_of((n - 1) * BLOCK, BLOCK)
        return block(q[pl.ds(q_lo, BLOCK), :], k[pl.ds(k_lo, 2 * BLOCK), :],
                     v[pl.ds(k_lo, 2 * BLOCK), :], bias_ref[g])

    emit(0, 0, first_block(0, q0.at[0], k0.at[0], v0.at[0]))

    def g0_body(n, carry):
        emit(0, pl.multiple_of(n * BLOCK, BLOCK), later_block(0, q0.at[0], k0.at[0], v0.at[0], n))
        return carry

    lax.fori_loop(1, s // BLOCK, g0_body, 0, unroll=5)

    for g, (q, k, v) in ((1, (q1, k1, v1)), (2, (q2, k2, v2))):
        d = B_DILATIONS[g]
        nb = s // d // BLOCK

        def res_body(res, carry, g=g, q=q, k=k, v=v, d=d, nb=nb):
            qr, kr, vr = q.at[0, res], k.at[0, res], v.at[0, res]
            emit(g, res, first_block(g, qr, kr, vr))
            if nb > 1:
                def blk_body(n, c2):
                    emit(g, res + n * (BLOCK * d), later_block(g, qr, kr, vr, n))
                    return c2

                lax.fori_loop(1, nb, blk_body, 0, unroll=True)
            return carry

        lax.fori_loop(0, d, res_body, 0, unroll=2 if nb > 1 else 4)

    def merge_body(c, carry):
        rows = pl.ds(pl.multiple_of(c * MERGE_ROWS, MERGE_ROWS), MERGE_ROWS)
        for j in range(SLABS):
            l0, l1, l2 = lse_scr[0, j, rows, :], lse_scr[1, j, rows, :], lse_scr[2, j, rows, :]
            lm = jnp.maximum(jnp.maximum(l0, l1), l2)
            e0, e1, e2 = jnp.exp(l0 - lm), jnp.exp(l1 - lm), jnp.exp(l2 - lm)
            y = (e0 * o_scr[0, j, rows, :] + e1 * o_scr[1, j, rows, :] + e2 * o_scr[2, j, rows, :]) / (e0 + e1 + e2)
            yb_ref[0, rows, j * LANES:(j + 1) * LANES] = y.astype(yb_ref.dtype)
        return carry

    lax.fori_loop(0, s // MERGE_ROWS, merge_body, 0)


def _dilated_attention(qkv, bias):
    bn, s, _ = qkv[0].shape
    in_specs = []
    for d in B_DILATIONS:
        if d == 1:
            spec = pl.BlockSpec((1, s, B_GROUP_WIDTH), lambda b: (b, 0, 0))
        else:
            spec = pl.BlockSpec((1, d, s // d, B_GROUP_WIDTH), lambda b: (b, 0, 0, 0))
        in_specs += [spec] * 3
    in_specs.append(_const_spec((B_GROUPS, B_HEADS_PER_GROUP * BLOCK, 2 * BLOCK)))
    return pl.pallas_call(
        functools.partial(_attn_kernel, s=s),
        grid=(bn,),
        in_specs=in_specs,
        out_specs=pl.BlockSpec((1, s, B_GROUP_WIDTH), lambda b: (b, 0, 0)),
        out_shape=jax.ShapeDtypeStruct((bn, s, B_GROUP_WIDTH), BF16),
        scratch_shapes=[pltpu.VMEM((B_GROUPS, SLABS, s, LANES), F32),
                        pltpu.VMEM((B_GROUPS, SLABS, s, LANES), F32)],
        compiler_params=pltpu.CompilerParams(
            dimension_semantics=("parallel",), vmem_limit_bytes=VMEM_LIMIT_BYTES),
        name="dilated_attn",
    )(*qkv, bias)


def _memkv_kernel(mem_ref, nw_ref, w_ref, kw_ref, mk_ref, mv_ref):
    hm = (_rms_rows(mem_ref[0]) * nw_ref[...]).astype(BF16)
    kv = _dot(hm, w_ref[...])
    for h in range(M_HEADS):
        cols = slice(h * M_HEAD_DIM, (h + 1) * M_HEAD_DIM)
        mk_ref[0, :, cols] = (_rms_rows(kv[:, cols]) * kw_ref[...]).astype(BF16)
    mv_ref[0] = kv[:, M_WIDTH:].astype(BF16)


def _mem_kv(mem, mem_norm_w, w_kv, k_norm_w):
    bn = mem.shape[0]
    out = jax.ShapeDtypeStruct((bn, MEM_LEN, M_WIDTH), BF16)
    ospec = pl.BlockSpec((1, MEM_LEN, M_WIDTH), lambda b: (b, 0, 0))
    return pl.pallas_call(
        _memkv_kernel,
        grid=(bn,),
        in_specs=[
            pl.BlockSpec((1, MEM_LEN, D_MODEL), lambda b: (b, 0, 0)),
            _const_spec((1, D_MODEL)),
            _const_spec((D_MODEL, 2 * M_WIDTH)),
            _const_spec((1, M_HEAD_DIM)),
        ],
        out_specs=[ospec, ospec],
        out_shape=[out, out],
        compiler_params=pltpu.CompilerParams(dimension_semantics=("parallel",)),
        name="mem_kv",
    )(mem, mem_norm_w, w_kv, k_norm_w)


MW_AU, MW_AV, MW_AZ = 0, 768, 1536
MW_BZ, MW_MQ, MW_MZ = 2304, 2560, 3072
MW_G = 3584
MW_TOTAL = MW_G + 3 * D_MODEL


def _main_kernel(x_ref, yb_ref, mk_ref, mv_ref,
                 nw_ref, w_ref, gb_ref, lnw_ref, lnb_ref, sw_ref, sb_ref, mqw_ref,
                 pa_ref, pb_ref, pm_ref, wo_ref, out_ref, *, tm):
    xf = x_ref[...]
    h = (_rms_rows(xf) * nw_ref[...]).astype(BF16)

    def proj(lo, width):
        return _dot(h, w_ref[:, lo:lo + width])

    def gate(i):
        return jax.nn.sigmoid(proj(MW_G + i * D_MODEL, D_MODEL) + gb_ref[i:i + 1, :])

    u = _gelu_exact(proj(MW_AU, A_WIDTH))
    gv = _gelu_exact(proj(MW_AV, A_WIDTH))
    mu = jnp.mean(gv, axis=-1, keepdims=True)
    gc = gv - mu
    vv = gc * lax.rsqrt(jnp.mean(gc * gc, axis=-1, keepdims=True) + EPS) * lnw_ref[...] + lnb_ref[...]
    vv = vv.astype(BF16)
    ti = lax.broadcasted_iota(jnp.int32, (CHUNK, CHUNK), 0)
    si = lax.broadcasted_iota(jnp.int32, (CHUNK, CHUNK), 1)
    causal = si <= ti
    ws = [jnp.where(causal, sw_ref[g], 0.0).astype(BF16) for g in range(A_GROUPS)]
    win_lo = [(g * A_GROUP_DIM) // LANES * LANES for g in range(A_GROUPS)]
    lane = lax.broadcasted_iota(jnp.int32, (1, LANES), 1)
    mixed_chunks = []
    for c in range(tm // CHUNK):
        vc = vv[c * CHUNK:(c + 1) * CHUNK, :]
        m = [_dot(ws[g], vc[:, win_lo[g]:win_lo[g] + 2 * LANES]) for g in range(A_GROUPS)]
        lo, hi = (lambda t: t[:, :LANES]), (lambda t: t[:, LANES:])
        in_first = lane < (A_GROUP_DIM - LANES)
        pieces = [lo(m[0]), jnp.where(in_first, hi(m[0]), lo(m[1])), hi(m[1]),
                  lo(m[2]), jnp.where(in_first, hi(m[2]), lo(m[3])), hi(m[3])]
        mixed_chunks.append(jnp.concatenate(pieces, axis=-1) + sb_ref[...])
    mixed = jnp.concatenate(mixed_chunks, axis=0)
    y_a = (u * mixed * jax.nn.silu(proj(MW_AZ, A_WIDTH))).astype(BF16)
    acc = gate(0) * _dot(y_a, pa_ref[...])

    mq = proj(MW_MQ, M_WIDTH)
    ym_heads = []
    for hd in range(M_HEADS):
        cols = slice(hd * M_HEAD_DIM, (hd + 1) * M_HEAD_DIM)
        qn = (_rms_rows(mq[:, cols]) * mqw_ref[...]).astype(BF16)
        sc = _dot_nt(qn, mk_ref[0, :, cols]) * (M_HEAD_DIM ** -0.5)
        sc = sc - jnp.max(sc, axis=-1, keepdims=True)
        p = jnp.exp(sc)
        l = jnp.sum(p, axis=-1, keepdims=True)
        ym_heads.append(_dot(p.astype(BF16), mv_ref[0, :, cols]) / l)
    y_m = (jnp.concatenate(ym_heads, axis=-1) * jax.nn.silu(proj(MW_MZ, M_WIDTH))).astype(BF16)
    acc = acc + gate(2) * _dot(y_m, pm_ref[...])

    y_b = (yb_ref[...].astype(F32) * jax.nn.silu(proj(MW_BZ, B_GROUP_WIDTH))).astype(BF16)
    acc = acc + gate(1) * _dot(y_b, pb_ref[...])

    out_ref[...] = xf + _dot(acc.astype(BF16), wo_ref[...])


def _main_block(x2, yb, mk, mv, norm_w, w_main, gate_b, lnw, lnb, sw, sb_full, mqw,
                proj_a, proj_b, proj_m, w_out, s):
    n = x2.shape[0]
    tm = MAIN_TM
    steps_per_batch = s // tm
    tile = lambda width: pl.BlockSpec((tm, width), lambda i: (i, 0))
    mem_spec = pl.BlockSpec((1, MEM_LEN, M_WIDTH), lambda i: (i // steps_per_batch, 0, 0))
    return pl.pallas_call(
        functools.partial(_main_kernel, tm=tm),
        grid=(n // tm,),
        in_specs=[tile(D_MODEL), tile(B_GROUP_WIDTH), mem_spec, mem_spec,
                  _const_spec((1, D_MODEL)),
                  _const_spec((D_MODEL, MW_TOTAL)),
                  _const_spec((3, D_MODEL)),
                  _const_spec((1, A_WIDTH)),
                  _const_spec((1, A_WIDTH)),
                  _const_spec((A_GROUPS, CHUNK, CHUNK)),
                  _const_spec((CHUNK, A_WIDTH)),
                  _const_spec((1, M_HEAD_DIM)),
                  _const_spec((A_WIDTH, D_MODEL)),
                  _const_spec((B_GROUP_WIDTH, D_MODEL)),
                  _const_spec((M_WIDTH, D_MODEL)),
                  _const_spec((D_MODEL, D_MODEL))],
        out_specs=tile(D_MODEL),
        out_shape=jax.ShapeDtypeStruct((n, D_MODEL), F32),
        compiler_params=pltpu.CompilerParams(
            dimension_semantics=("parallel",), vmem_limit_bytes=VMEM_LIMIT_BYTES),
        name="main_block",
    )(x2, yb, mk, mv, norm_w, w_main, gate_b, lnw, lnb, sw, sb_full, mqw,
      proj_a, proj_b, proj_m, w_out)


def kernel(x, mem, norm_w, w_in, gate_b, a_v_norm_w, a_v_norm_b, a_spatial_w, a_spatial_b,
           b_q_norm_w, b_k_norm_w, rel_bias, mem_norm_w, m_w_kv, m_q_norm_w, m_k_norm_w,
           proj_a, proj_b, proj_m, w_out):
    bn, s, _ = x.shape
    n = bn * s
    row = lambda v: v.reshape(1, -1)

    w_qkv = w_in[:, OFF_BQ:OFF_BZ].astype(BF16)
    w_main = jnp.concatenate([w_in[:, :OFF_BQ], w_in[:, OFF_BZ:]], axis=1).astype(BF16)
    wq_t = row(jnp.tile(b_q_norm_w * (B_HEAD_DIM ** -0.5), B_HEADS_PER_GROUP))
    wk_t = row(jnp.tile(b_k_norm_w, B_HEADS_PER_GROUP))
    sb_full = jnp.repeat(a_spatial_b.T, A_GROUP_DIM, axis=1)

    bias = _expand_bias(rel_bias)
    qkv = _qkv_proj(x, row(norm_w), w_qkv, wq_t, wk_t)
    yb = _dilated_attention(qkv, bias)
    mk, mv = _mem_kv(mem, row(mem_norm_w), m_w_kv.astype(BF16), row(m_k_norm_w))
    out = _main_block(x.reshape(n, D_MODEL), yb.reshape(n, B_GROUP_WIDTH), mk, mv, row(norm_w), w_main, gate_b,
                      row(a_v_norm_w), row(a_v_norm_b), a_spatial_w, sb_full, row(m_q_norm_w),
                      proj_a.astype(BF16), proj_b.astype(BF16), proj_m.astype(BF16), w_out.astype(BF16), s)
    return out.reshape(bn, s, D_MODEL)
```

```python
import functools
import math

import jax
import jax.numpy as jnp
from jax import lax
from jax.experimental import pallas as pl
from jax.experimental.pallas import tpu as pltpu

EPS = 1e-6
D_MODEL = 1024
A_WIDTH = 768
A_GROUPS = 4
A_GROUP_DIM = A_WIDTH // A_GROUPS
CHUNK = 128
B_PATTERNS = ((128, 1), (512, 4), (2048, 16))
B_DILATIONS = tuple(d for _, d in B_PATTERNS)
B_GROUPS = 3
B_HEADS_PER_GROUP = 4
B_HEAD_DIM = 64
B_GROUP_WIDTH = B_HEADS_PER_GROUP * B_HEAD_DIM
B_QKV_WIDTH = B_GROUPS * B_GROUP_WIDTH
BLOCK = 128
MEM_LEN = 256
M_HEADS = 4
M_HEAD_DIM = 128
M_WIDTH = M_HEADS * M_HEAD_DIM
REL_BUCKETS = 32
REL_MAX_DISTANCE = 2048
NEG = -1e30
LOG2E = math.log2(math.e)

OFF_BQ = 2304
OFF_BZ = 4608

LANES = 128
SLABS = B_GROUP_WIDTH // LANES
VMEM_LIMIT_BYTES = 56 * 1024 * 1024

QKV_TM = 1024
MAIN_TM = 512
MERGE_ROWS = 256

BF16 = jnp.bfloat16
F32 = jnp.float32


def _dot(a, b):
    return jnp.dot(a, b, preferred_element_type=F32)


def _dot_nt(a, b):
    return lax.dot_general(a, b, (((1,), (1,)), ((), ())), preferred_element_type=F32)


def _rms_rows(xf):
    return xf * lax.rsqrt(jnp.mean(xf * xf, axis=-1, keepdims=True) + EPS)


def _gelu_exact(t):
    return 0.5 * t * (1.0 + lax.erf(t * (2.0 ** -0.5)))


def _const_spec(shape):
    nd = len(shape)
    return pl.BlockSpec(shape, lambda *_: (0,) * nd, pipeline_mode=pl.Buffered(1))


def _t5_causal_bucket(dist):
    max_exact = REL_BUCKETS // 2
    is_small = dist < max_exact
    df = jnp.maximum(dist, 1).astype(F32)
    large = max_exact + (jnp.log(df / max_exact) / math.log(REL_MAX_DISTANCE / max_exact)
                         * (REL_BUCKETS - max_exact)).astype(jnp.int32)
    large = jnp.minimum(large, REL_BUCKETS - 1)
    return jnp.where(is_small, dist, large)


def _bias_kernel(bucket_ref, valid_ref, rb_ref, out_ref):
    g = pl.program_id(0)
    bk = bucket_ref[0]
    valid = valid_ref[...] != 0
    for h in range(B_HEADS_PER_GROUP):
        acc = jnp.zeros((BLOCK, 2 * BLOCK), F32)
        for b in range(REL_BUCKETS):
            acc = jnp.where(bk == b, rb_ref[b, g * B_HEADS_PER_GROUP + h], acc)
        out_ref[0, h * BLOCK:(h + 1) * BLOCK, :] = jnp.where(valid, acc * LOG2E, NEG)


def _expand_bias(rel_bias):
    qi = jnp.arange(BLOCK, dtype=jnp.int32)[:, None]
    kj = jnp.arange(2 * BLOCK, dtype=jnp.int32)[None, :]
    step = qi + BLOCK - kj
    buckets = jnp.stack([_t5_causal_bucket(jnp.maximum(step, 0) * d) for d in B_DILATIONS])
    valid = ((step >= 0) & (step <= BLOCK)).astype(jnp.int32)
    return pl.pallas_call(
        _bias_kernel,
        grid=(B_GROUPS,),
        in_specs=[
            pl.BlockSpec((1, BLOCK, 2 * BLOCK), lambda g: (g, 0, 0)),
            pl.BlockSpec((BLOCK, 2 * BLOCK), lambda g: (0, 0)),
            pl.BlockSpec(memory_space=pltpu.SMEM),
        ],
        out_specs=pl.BlockSpec((1, B_HEADS_PER_GROUP * BLOCK, 2 * BLOCK), lambda g: (g, 0, 0)),
        out_shape=jax.ShapeDtypeStruct((B_GROUPS, B_HEADS_PER_GROUP * BLOCK, 2 * BLOCK), F32),
        name="rel_bias_expand",
    )(buckets, valid, rel_bias)


def _qkv_kernel(x_ref, nw_ref, w_ref, wq_ref, wk_ref, *refs, tm):
    out_refs = refs[:3 * B_GROUPS]
    scratch = refs[3 * B_GROUPS:]
    h = (_rms_rows(x_ref[0]) * nw_ref[...]).astype(BF16)
    r = lax.broadcasted_iota(jnp.int32, (B_GROUP_WIDTH, B_GROUP_WIDTH), 0) // B_HEAD_DIM
    c = lax.broadcasted_iota(jnp.int32, (B_GROUP_WIDTH, B_GROUP_WIDTH), 1) // B_HEAD_DIM
    seg = (r == c).astype(BF16)

    def head_norm(t, w):
        ms = _dot((t * t).astype(BF16), seg) * (1.0 / B_HEAD_DIM)
        return t * lax.rsqrt(ms + EPS) * w

    def emit(g, which, t):
        out = out_refs[3 * g + which]
        d = B_DILATIONS[g]
        if d == 1:
            out[0] = t.astype(BF16)
            return
        scr = scratch[3 * (g - 1) + which]
        for j in range(SLABS):
            scr[j] = t[:, j * LANES:(j + 1) * LANES]
        for res in range(d):
            for j in range(SLABS):
                out[0, res, :, j * LANES:(j + 1) * LANES] = (
                    scr[j, pl.ds(res, tm // d, stride=d), :].astype(BF16))

    def piece(g, which):
        lo = which * B_QKV_WIDTH + g * B_GROUP_WIDTH
        return _dot(h, w_ref[:, lo:lo + B_GROUP_WIDTH])

    for g in range(B_GROUPS):
        emit(g, 0, head_norm(piece(g, 0), wq_ref[...]))
        emit(g, 1, head_norm(piece(g, 1), wk_ref[...]))
        emit(g, 2, piece(g, 2))


def _qkv_proj(x, norm_w, w_qkv, wq_t, wk_t):
    bn, s, _ = x.shape
    tm = QKV_TM
    out_shapes, out_specs, scratch = [], [], []
    for d in B_DILATIONS:
        if d == 1:
            shape = jax.ShapeDtypeStruct((bn, s, B_GROUP_WIDTH), BF16)
            spec = pl.BlockSpec((1, tm, B_GROUP_WIDTH), lambda b, j: (b, j, 0))
        else:
            shape = jax.ShapeDtypeStruct((bn, d, s // d, B_GROUP_WIDTH), BF16)
            spec = pl.BlockSpec((1, d, tm // d, B_GROUP_WIDTH), lambda b, j: (b, 0, j, 0))
            scratch += [pltpu.VMEM((SLABS, tm, LANES), F32)] * 3
        out_shapes += [shape] * 3
        out_specs += [spec] * 3
    return pl.pallas_call(
        functools.partial(_qkv_kernel, tm=tm),
        grid=(bn, s // tm),
        in_specs=[
            pl.BlockSpec((1, tm, D_MODEL), lambda b, j: (b, j, 0)),
            _const_spec((1, D_MODEL)),
            _const_spec((D_MODEL, 3 * B_QKV_WIDTH)),
            _const_spec((1, B_GROUP_WIDTH)),
            _const_spec((1, B_GROUP_WIDTH)),
        ],
        out_specs=out_specs,
        out_shape=out_shapes,
        scratch_shapes=scratch,
        compiler_params=pltpu.CompilerParams(
            dimension_semantics=("parallel", "parallel"), vmem_limit_bytes=VMEM_LIMIT_BYTES),
        name="qkv_proj",
    )(x, norm_w, w_qkv, wq_t, wk_t)


def _attn_kernel(q0, k0, v0, q1, k1, v1, q2, k2, v2, bias_ref, yb_ref, o_scr, lse_scr, *, s):
    lane_head = lax.broadcasted_iota(jnp.int32, (1, B_GROUP_WIDTH), 1) // B_HEAD_DIM
    masks = [lane_head == h for h in range(B_HEADS_PER_GROUP)]

    def block(qb, kb, vb, bias):
        qs = jnp.concatenate([jnp.where(m, qb, jnp.zeros_like(qb)) for m in masks], axis=0)
        sc = _dot_nt(qs, kb) + bias
        mx = jnp.max(sc, axis=-1, keepdims=True)
        p = jnp.exp2(sc - mx)
        l = jnp.sum(p, axis=-1, keepdims=True)
        o = _dot(p.astype(BF16), vb) / l
        lse = mx + jnp.log2(l)
        o_out = jnp.zeros((BLOCK, B_GROUP_WIDTH), F32)
        lse_out = jnp.zeros((BLOCK, B_GROUP_WIDTH), F32)
        for h, m in enumerate(masks):
            rows = slice(h * BLOCK, (h + 1) * BLOCK)
            o_out = jnp.where(m, o[rows], o_out)
            lse_out = jnp.where(m, lse[rows], lse_out)
        return o_out, lse_out

    def emit(g, start, res):
        d = B_DILATIONS[g]
        idx = pl.ds(start, BLOCK) if d == 1 else pl.ds(start, BLOCK, stride=d)
        o_out, lse_out = res
        for j in range(SLABS):
            o_scr[g, j, idx, :] = o_out[:, j * LANES:(j + 1) * LANES]
            lse_scr[g, j, idx, :] = lse_out[:, j * LANES:(j + 1) * LANES]

    def first_block(g, q, k, v):
        return block(q[0:BLOCK, :], k[0:BLOCK, :], v[0:BLOCK, :], bias_ref[g, :, BLOCK:2 * BLOCK])

    def later_block(g, q, k, v, n):
        q_lo = pl.multiple_of(n * BLOCK, BLOCK)
        k_lo = pl.multiple_of((n - 1) * BLOCK, BLOCK)
        return block(q[pl.ds(q_lo, BLOCK), :], k[pl.ds(k_lo, 2 * BLOCK), :],
                     v[pl.ds(k_lo, 2 * BLOCK), :], bias_ref[g])

    emit(0, 0, first_block(0, q0.at[0], k0.at[0], v0.at[0]))

    def g0_body(n, carry):
        emit(0, pl.multiple_of(n * BLOCK, BLOCK), later_block(0, q0.at[0], k0.at[0], v0.at[0], n))
        return carry

    lax.fori_loop(1, s // BLOCK, g0_body, 0, unroll=5)

    for g, (q, k, v) in ((1, (q1, k1, v1)), (2, (q2, k2, v2))):
        d = B_DILATIONS[g]
        nb = s // d // BLOCK

        def res_body(res, carry, g=g, q=q, k=k, v=v, d=d, nb=nb):
            qr, kr, vr = q.at[0, res], k.at[0, res], v.at[0, res]
            emit(g, res, first_block(g, qr, kr, vr))
            if nb > 1:
                def blk_body(n, c2):
                    emit(g, res + n * (BLOCK * d), later_block(g, qr, kr, vr, n))
                    return c2

                lax.fori_loop(1, nb, blk_body, 0, unroll=True)
            return carry

        lax.fori_loop(0, d, res_body, 0, unroll=2 if nb > 1 else 4)

    def merge_body(c, carry):
        rows = pl.ds(pl.multiple_of(c * MERGE_ROWS, MERGE_ROWS), MERGE_ROWS)
        for j in range(SLABS):
            l0, l1, l2 = lse_scr[0, j, rows, :], lse_scr[1, j, rows, :], lse_scr[2, j, rows, :]
            lm = jnp.maximum(jnp.maximum(l0, l1), l2)
            e0, e1, e2 = jnp.exp2(l0 - lm), jnp.exp2(l1 - lm), jnp.exp2(l2 - lm)
            y = (e0 * o_scr[0, j, rows, :] + e1 * o_scr[1, j, rows, :] + e2 * o_scr[2, j, rows, :]) / (e0 + e1 + e2)
            yb_ref[0, rows, j * LANES:(j + 1) * LANES] = y.astype(yb_ref.dtype)
        return carry

    lax.fori_loop(0, s // MERGE_ROWS, merge_body, 0)


def _dilated_attention(qkv, bias):
    bn, s, _ = qkv[0].shape
    in_specs = []
    for d in B_DILATIONS:
        if d == 1:
            spec = pl.BlockSpec((1, s, B_GROUP_WIDTH), lambda b: (b, 0, 0))
        else:
            spec = pl.BlockSpec((1, d, s // d, B_GROUP_WIDTH), lambda b: (b, 0, 0, 0))
        in_specs += [spec] * 3
    in_specs.append(_const_spec((B_GROUPS, B_HEADS_PER_GROUP * BLOCK, 2 * BLOCK)))
    return pl.pallas_call(
        functools.partial(_attn_kernel, s=s),
        grid=(bn,),
        in_specs=in_specs,
        out_specs=pl.BlockSpec((1, s, B_GROUP_WIDTH), lambda b: (b, 0, 0)),
        out_shape=jax.ShapeDtypeStruct((bn, s, B_GROUP_WIDTH), BF16),
        scratch_shapes=[pltpu.VMEM((B_GROUPS, SLABS, s, LANES), F32),
                        pltpu.VMEM((B_GROUPS, SLABS, s, LANES), F32)],
        compiler_params=pltpu.CompilerParams(
            dimension_semantics=("parallel",), vmem_limit_bytes=VMEM_LIMIT_BYTES),
        name="dilated_attn",
    )(*qkv, bias)


def _memkv_kernel(mem_ref, nw_ref, w_ref, kw_ref, mk_ref, mv_ref):
    hm = (_rms_rows(mem_ref[0]) * nw_ref[...]).astype(BF16)
    kv = _dot(hm, w_ref[...])
    for h in range(M_HEADS):
        cols = slice(h * M_HEAD_DIM, (h + 1) * M_HEAD_DIM)
        mk_ref[0, :, cols] = (_rms_rows(kv[:, cols]) * kw_ref[...]).astype(BF16)
    mv_ref[0] = kv[:, M_WIDTH:].astype(BF16)


def _mem_kv(mem, mem_norm_w, w_kv, k_norm_w):
    bn = mem.shape[0]
    out = jax.ShapeDtypeStruct((bn, MEM_LEN, M_WIDTH), BF16)
    ospec = pl.BlockSpec((1, MEM_LEN, M_WIDTH), lambda b: (b, 0, 0))
    return pl.pallas_call(
        _memkv_kernel,
        grid=(bn,),
        in_specs=[
            pl.BlockSpec((1, MEM_LEN, D_MODEL), lambda b: (b, 0, 0)),
            _const_spec((1, D_MODEL)),
            _const_spec((D_MODEL, 2 * M_WIDTH)),
            _const_spec((1, M_HEAD_DIM)),
        ],
        out_specs=[ospec, ospec],
        out_shape=[out, out],
        compiler_params=pltpu.CompilerParams(dimension_semantics=("parallel",)),
        name="mem_kv",
    )(mem, mem_norm_w, w_kv, k_norm_w)


MW_AU, MW_AV, MW_AZ = 0, 768, 1536
MW_BZ, MW_MQ, MW_MZ = 2304, 2560, 3072
MW_G = 3584
MW_TOTAL = MW_G + 3 * D_MODEL


def _main_kernel(x_ref, yb_ref, mk_ref, mv_ref,
                 nw_ref, w_ref, gb_ref, lnw_ref, lnb_ref, sw_ref, sb_ref, mqw_ref,
                 pa_ref, pb_ref, pm_ref, wo_ref, out_ref, *, tm):
    xf = x_ref[...]
    h = (_rms_rows(xf) * nw_ref[...]).astype(BF16)

    def proj(lo, width):
        return _dot(h, w_ref[:, lo:lo + width])

    def gate(i):
        return jax.nn.sigmoid(proj(MW_G + i * D_MODEL, D_MODEL) + gb_ref[i:i + 1, :])

    u = _gelu_exact(proj(MW_AU, A_WIDTH))
    gv = _gelu_exact(proj(MW_AV, A_WIDTH))
    mu = jnp.mean(gv, axis=-1, keepdims=True)
    gc = gv - mu
    vv = gc * lax.rsqrt(jnp.mean(gc * gc, axis=-1, keepdims=True) + EPS) * lnw_ref[...] + lnb_ref[...]
    vv = vv.astype(BF16)
    ti = lax.broadcasted_iota(jnp.int32, (CHUNK, CHUNK), 0)
    si = lax.broadcasted_iota(jnp.int32, (CHUNK, CHUNK), 1)
    causal = si <= ti
    ws = [jnp.where(causal, sw_ref[g], 0.0).astype(BF16) for g in range(A_GROUPS)]
    win_lo = [(g * A_GROUP_DIM) // LANES * LANES for g in range(A_GROUPS)]
    lane = lax.broadcasted_iota(jnp.int32, (1, LANES), 1)
    mixed_chunks = []
    for c in range(tm // CHUNK):
        vc = vv[c * CHUNK:(c + 1) * CHUNK, :]
        m = [_dot(ws[g], vc[:, win_lo[g]:win_lo[g] + 2 * LANES]) for g in range(A_GROUPS)]
        lo, hi = (lambda t: t[:, :LANES]), (lambda t: t[:, LANES:])
        in_first = lane < (A_GROUP_DIM - LANES)
        pieces = [lo(m[0]), jnp.where(in_first, hi(m[0]), lo(m[1])), hi(m[1]),
                  lo(m[2]), jnp.where(in_first, hi(m[2]), lo(m[3])), hi(m[3])]
        mixed_chunks.append(jnp.concatenate(pieces, axis=-1) + sb_ref[...])
    mixed = jnp.concatenate(mixed_chunks, axis=0)
    y_a = (u * mixed * jax.nn.silu(proj(MW_AZ, A_WIDTH))).astype(BF16)
    acc = gate(0) * _dot(y_a, pa_ref[...])

    mq = proj(MW_MQ, M_WIDTH)
    ym_heads = []
    for hd in range(M_HEADS):
        cols = slice(hd * M_HEAD_DIM, (hd + 1) * M_HEAD_DIM)
        qn = (_rms_rows(mq[:, cols]) * mqw_ref[...]).astype(BF16)
        sc = _dot_nt(qn, mk_ref[0, :, cols]) * (M_HEAD_DIM ** -0.5)
        sc = sc - jnp.max(sc, axis=-1, keepdims=True)
        p = jnp.exp(sc)
        l = jnp.sum(p, axis=-1, keepdims=True)
        ym_heads.append(_dot(p.astype(BF16), mv_ref[0, :, cols]) / l)
    y_m = (jnp.concatenate(ym_heads, axis=-1) * jax.nn.silu(proj(MW_MZ, M_WIDTH))).astype(BF16)
    acc = acc + gate(2) * _dot(y_m, pm_ref[...])

    y_b = (yb_ref[...].astype(F32) * jax.nn.silu(proj(MW_BZ, B_GROUP_WIDTH))).astype(BF16)
    acc = acc + gate(1) * _dot(y_b, pb_ref[...])

    out_ref[...] = xf + _dot(acc.astype(BF16), wo_ref[...])


def _main_block(x2, yb, mk, mv, norm_w, w_main, gate_b, lnw, lnb, sw, sb_full, mqw,
                proj_a, proj_b, proj_m, w_out, s):
    n = x2.shape[0]
    tm = MAIN_TM
    steps_per_batch = s // tm
    tile = lambda width: pl.BlockSpec((tm, width), lambda i: (i, 0))
    mem_spec = pl.BlockSpec((1, MEM_LEN, M_WIDTH), lambda i: (i // steps_per_batch, 0, 0))
    return pl.pallas_call(
        functools.partial(_main_kernel, tm=tm),
        grid=(n // tm,),
        in_specs=[tile(D_MODEL), tile(B_GROUP_WIDTH), mem_spec, mem_spec,
                  _const_spec((1, D_MODEL)),
                  _const_spec((D_MODEL, MW_TOTAL)),
                  _const_spec((3, D_MODEL)),
                  _const_spec((1, A_WIDTH)),
                  _const_spec((1, A_WIDTH)),
                  _const_spec((A_GROUPS, CHUNK, CHUNK)),
                  _const_spec((CHUNK, A_WIDTH)),
                  _const_spec((1, M_HEAD_DIM)),
                  _const_spec((A_WIDTH, D_MODEL)),
                  _const_spec((B_GROUP_WIDTH, D_MODEL)),
                  _const_spec((M_WIDTH, D_MODEL)),
                  _const_spec((D_MODEL, D_MODEL))],
        out_specs=tile(D_MODEL),
        out_shape=jax.ShapeDtypeStruct((n, D_MODEL), F32),
        compiler_params=pltpu.CompilerParams(
            dimension_semantics=("parallel",), vmem_limit_bytes=VMEM_LIMIT_BYTES),
        name="main_block",
    )(x2, yb, mk, mv, norm_w, w_main, gate_b, lnw, lnb, sw, sb_full, mqw,
      proj_a, proj_b, proj_m, w_out)


def kernel(x, mem, norm_w, w_in, gate_b, a_v_norm_w, a_v_norm_b, a_spatial_w, a_spatial_b,
           b_q_norm_w, b_k_norm_w, rel_bias, mem_norm_w, m_w_kv, m_q_norm_w, m_k_norm_w,
           proj_a, proj_b, proj_m, w_out):
    bn, s, _ = x.shape
    n = bn * s
    row = lambda v: v.reshape(1, -1)

    w_qkv = w_in[:, OFF_BQ:OFF_BZ].astype(BF16)
    w_main = jnp.concatenate([w_in[:, :OFF_BQ], w_in[:, OFF_BZ:]], axis=1).astype(BF16)
    wq_t = row(jnp.tile(b_q_norm_w * (B_HEAD_DIM ** -0.5 * LOG2E), B_HEADS_PER_GROUP))
    wk_t = row(jnp.tile(b_k_norm_w, B_HEADS_PER_GROUP))
    sb_full = jnp.repeat(a_spatial_b.T, A_GROUP_DIM, axis=1)

    bias = _expand_bias(rel_bias)
    qkv = _qkv_proj(x, row(norm_w), w_qkv, wq_t, wk_t)
    yb = _dilated_attention(qkv, bias)
    mk, mv = _mem_kv(mem, row(mem_norm_w), m_w_kv.astype(BF16), row(m_k_norm_w))
    out = _main_block(x.reshape(n, D_MODEL), yb.reshape(n, B_GROUP_WIDTH), mk, mv, row(norm_w), w_main, gate_b,
                      row(a_v_norm_w), row(a_v_norm_b), a_spatial_w, sb_full, row(m_q_norm_w),
                      proj_a.astype(BF16), proj_b.astype(BF16), proj_m.astype(BF16), w_out.astype(BF16), s)
    return out.reshape(bn, s, D_MODEL)
```

```python
import functools
import math

import jax
import jax.numpy as jnp
import numpy as np
from jax import lax
from jax.experimental import pallas as pl
from jax.experimental.pallas import tpu as pltpu

EPS = 1e-6
D_MODEL = 1024
A_WIDTH = 768
A_GROUPS = 4
A_GROUP_DIM = A_WIDTH // A_GROUPS
CHUNK = 128
B_PATTERNS = ((128, 1), (512, 4), (2048, 16))
B_DILATIONS = tuple(d for _, d in B_PATTERNS)
B_GROUPS = 3
B_HEADS_PER_GROUP = 4
B_HEAD_DIM = 64
B_GROUP_WIDTH = B_HEADS_PER_GROUP * B_HEAD_DIM
B_QKV_WIDTH = B_GROUPS * B_GROUP_WIDTH
BLOCK = 128
MEM_LEN = 256
M_HEADS = 4
M_HEAD_DIM = 128
M_WIDTH = M_HEADS * M_HEAD_DIM
REL_BUCKETS = 32
REL_MAX_DISTANCE = 2048
NEG = -1e30
LOG2E = math.log2(math.e)

OFF_BQ = 2304
OFF_BZ = 4608

LANES = 128
SLABS = B_GROUP_WIDTH // LANES
VMEM_LIMIT_BYTES = 56 * 1024 * 1024

QKV_TM = 1024
MAIN_TM = 1024
MERGE_ROWS = 256
MEMKV_BATCH = 4
assert OFF_BQ % (OFF_BZ - OFF_BQ) == 0

BF16 = jnp.bfloat16
F32 = jnp.float32


def _dot(a, b):
    return jnp.dot(a, b, preferred_element_type=F32)


def _dot_nt(a, b):
    return lax.dot_general(a, b, (((1,), (1,)), ((), ())), preferred_element_type=F32)


def _rms_rows(xf):
    return xf * lax.rsqrt(jnp.mean(xf * xf, axis=-1, keepdims=True) + EPS)


def _gelu_exact(t):
    return 0.5 * t * (1.0 + lax.erf(t * (2.0 ** -0.5)))


def _const_spec(shape):
    nd = len(shape)
    return pl.BlockSpec(shape, lambda *_: (0,) * nd, pipeline_mode=pl.Buffered(1))


def _t5_causal_bucket(dist):
    max_exact = REL_BUCKETS // 2
    df = np.maximum(dist, 1).astype(np.float32)
    scaled = (np.log(df / np.float32(max_exact)) / np.float32(math.log(REL_MAX_DISTANCE / max_exact))
              * np.float32(REL_BUCKETS - max_exact))
    large = np.minimum(max_exact + scaled.astype(np.int32), REL_BUCKETS - 1)
    return np.where(dist < max_exact, dist, large).astype(np.int32)


def _bias_kernel(bucket_ref, valid_ref, rb_ref, out_ref):
    g = pl.program_id(0)
    bk = bucket_ref[0]
    valid = valid_ref[...] != 0
    for h in range(B_HEADS_PER_GROUP):
        acc = jnp.zeros((BLOCK, 2 * BLOCK), F32)
        for b in range(REL_BUCKETS):
            acc = jnp.where(bk == b, rb_ref[b, g * B_HEADS_PER_GROUP + h], acc)
        out_ref[0, h * BLOCK:(h + 1) * BLOCK, :] = jnp.where(valid, acc * LOG2E, NEG)


def _expand_bias(rel_bias):
    qi = np.arange(BLOCK, dtype=np.int32)[:, None]
    kj = np.arange(2 * BLOCK, dtype=np.int32)[None, :]
    step = qi + BLOCK - kj
    buckets = np.stack([_t5_causal_bucket(np.maximum(step, 0) * d) for d in B_DILATIONS])
    valid = ((step >= 0) & (step <= BLOCK)).astype(np.int32)
    return pl.pallas_call(
        _bias_kernel,
        grid=(B_GROUPS,),
        in_specs=[
            pl.BlockSpec((1, BLOCK, 2 * BLOCK), lambda g: (g, 0, 0)),
            pl.BlockSpec((BLOCK, 2 * BLOCK), lambda g: (0, 0)),
            pl.BlockSpec(memory_space=pltpu.SMEM),
        ],
        out_specs=pl.BlockSpec((1, B_HEADS_PER_GROUP * BLOCK, 2 * BLOCK), lambda g: (g, 0, 0)),
        out_shape=jax.ShapeDtypeStruct((B_GROUPS, B_HEADS_PER_GROUP * BLOCK, 2 * BLOCK), F32),
        name="rel_bias_expand",
    )(buckets, valid, rel_bias)


def _qkv_kernel(x_ref, nw_ref, w_ref, wq_ref, wk_ref, *refs, tm):
    out_refs = refs[:3 * B_GROUPS]
    scratch = refs[3 * B_GROUPS:]
    h = (_rms_rows(x_ref[0]) * nw_ref[...]).astype(BF16)
    r = lax.broadcasted_iota(jnp.int32, (B_GROUP_WIDTH, B_GROUP_WIDTH), 0) // B_HEAD_DIM
    c = lax.broadcasted_iota(jnp.int32, (B_GROUP_WIDTH, B_GROUP_WIDTH), 1) // B_HEAD_DIM
    seg = (r == c).astype(BF16)

    def head_norm(t, w):
        ms = _dot((t * t).astype(BF16), seg) * (1.0 / B_HEAD_DIM)
        return t * lax.rsqrt(ms + EPS) * w

    def emit(g, which, t):
        out = out_refs[3 * g + which]
        d = B_DILATIONS[g]
        if d == 1:
            out[0] = t.astype(BF16)
            return
        scr = scratch[3 * (g - 1) + which]
        for j in range(SLABS):
            scr[j] = t[:, j * LANES:(j + 1) * LANES]
        for res in range(d):
            for j in range(SLABS):
                out[0, res, :, j * LANES:(j + 1) * LANES] = (
                    scr[j, pl.ds(res, tm // d, stride=d), :].astype(BF16))

    def piece(g, which):
        lo = which * B_QKV_WIDTH + g * B_GROUP_WIDTH
        return _dot(h, w_ref[:, lo:lo + B_GROUP_WIDTH])

    for g in range(B_GROUPS):
        emit(g, 0, head_norm(piece(g, 0), wq_ref[...]))
        emit(g, 1, head_norm(piece(g, 1), wk_ref[...]))
        emit(g, 2, piece(g, 2))


def _qkv_proj(x, norm_w, w_qkv, wq_t, wk_t):
    bn, s, _ = x.shape
    tm = QKV_TM
    out_shapes, out_specs, scratch = [], [], []
    for d in B_DILATIONS:
        if d == 1:
            shape = jax.ShapeDtypeStruct((bn, s, B_GROUP_WIDTH), BF16)
            spec = pl.BlockSpec((1, tm, B_GROUP_WIDTH), lambda b, j: (b, j, 0))
        else:
            shape = jax.ShapeDtypeStruct((bn, d, s // d, B_GROUP_WIDTH), BF16)
            spec = pl.BlockSpec((1, d, tm // d, B_GROUP_WIDTH), lambda b, j: (b, 0, j, 0))
            scratch += [pltpu.VMEM((SLABS, tm, LANES), F32)] * 3
        out_shapes += [shape] * 3
        out_specs += [spec] * 3
    return pl.pallas_call(
        functools.partial(_qkv_kernel, tm=tm),
        grid=(bn, s // tm),
        in_specs=[
            pl.BlockSpec((1, tm, D_MODEL), lambda b, j: (b, j, 0)),
            _const_spec((1, D_MODEL)),
            pl.BlockSpec((D_MODEL, OFF_BZ - OFF_BQ), lambda b, j: (0, OFF_BQ // (OFF_BZ - OFF_BQ)),
                         pipeline_mode=pl.Buffered(1)),
            _const_spec((1, B_GROUP_WIDTH)),
            _const_spec((1, B_GROUP_WIDTH)),
        ],
        out_specs=out_specs,
        out_shape=out_shapes,
        scratch_shapes=scratch,
        compiler_params=pltpu.CompilerParams(
            dimension_semantics=("parallel", "parallel"), vmem_limit_bytes=VMEM_LIMIT_BYTES),
        name="qkv_proj",
    )(x, norm_w, w_qkv, wq_t, wk_t)


def _attn_kernel(q0, k0, v0, q1, k1, v1, q2, k2, v2, bias_ref, yb_ref, o_scr, lse_scr, *, s):
    lane_head = lax.broadcasted_iota(jnp.int32, (1, B_GROUP_WIDTH), 1) // B_HEAD_DIM
    masks = [lane_head == h for h in range(B_HEADS_PER_GROUP)]

    def block(qb, kb, vb, bias):
        qs = jnp.concatenate([jnp.where(m, qb, jnp.zeros_like(qb)) for m in masks], axis=0)
        sc = _dot_nt(qs, kb) + bias
        mx = jnp.max(sc, axis=-1, keepdims=True)
        p = jnp.exp2(sc - mx)
        l = jnp.sum(p, axis=-1, keepdims=True)
        o = _dot(p.astype(BF16), vb) / l
        lse = mx + jnp.log2(l)
        o_out = jnp.zeros((BLOCK, B_GROUP_WIDTH), F32)
        lse_out = jnp.zeros((BLOCK, B_GROUP_WIDTH), F32)
        for h, m in enumerate(masks):
            rows = slice(h * BLOCK, (h + 1) * BLOCK)
            o_out = jnp.where(m, o[rows], o_out)
            lse_out = jnp.where(m, lse[rows], lse_out)
        return o_out, lse_out

    def emit(g, start, res):
        d = B_DILATIONS[g]
        idx = pl.ds(start, BLOCK) if d == 1 else pl.ds(start, BLOCK, stride=d)
        o_out, lse_out = res
        for j in range(SLABS):
            o_scr[g, j, idx, :] = o_out[:, j * LANES:(j + 1) * LANES]
            lse_scr[g, j, idx, :] = lse_out[:, j * LANES:(j + 1) * LANES]

    def first_block(g, q, k, v):
        return block(q[0:BLOCK, :], k[0:BLOCK, :], v[0:BLOCK, :], bias_ref[g, :, BLOCK:2 * BLOCK])

    def later_block(g, q, k, v, n):
        q_lo = pl.multiple_of(n * BLOCK, BLOCK)
        k_lo = pl.multiple_of((n - 1) * BLOCK, BLOCK)
        return block(q[pl.ds(q_lo, BLOCK), :], k[pl.ds(k_lo, 2 * BLOCK), :],
                     v[pl.ds(k_lo, 2 * BLOCK), :], bias_ref[g])

    emit(0, 0, first_block(0, q0.at[0], k0.at[0], v0.at[0]))

    def g0_body(n, carry):
        emit(0, pl.multiple_of(n * BLOCK, BLOCK), later_block(0, q0.at[0], k0.at[0], v0.at[0], n))
        return carry

    lax.fori_loop(1, s // BLOCK, g0_body, 0, unroll=5)

    for g, (q, k, v) in ((1, (q1, k1, v1)), (2, (q2, k2, v2))):
        d = B_DILATIONS[g]
        nb = s // d // BLOCK

        def res_body(res, carry, g=g, q=q, k=k, v=v, d=d, nb=nb):
            qr, kr, vr = q.at[0, res], k.at[0, res], v.at[0, res]
            emit(g, res, first_block(g, qr, kr, vr))
            if nb > 1:
                def blk_body(n, c2):
                    emit(g, res + n * (BLOCK * d), later_block(g, qr, kr, vr, n))
                    return c2

                lax.fori_loop(1, nb, blk_body, 0, unroll=True)
            return carry

        lax.fori_loop(0, d, res_body, 0, unroll=2 if nb > 1 else 4)

    def merge_body(c, carry):
        rows = pl.ds(pl.multiple_of(c * MERGE_ROWS, MERGE_ROWS), MERGE_ROWS)
        for j in range(SLABS):
            l0, l1, l2 = lse_scr[0, j, rows, :], lse_scr[1, j, rows, :], lse_scr[2, j, rows, :]
            lm = jnp.maximum(jnp.maximum(l0, l1), l2)
            e0, e1, e2 = jnp.exp2(l0 - lm), jnp.exp2(l1 - lm), jnp.exp2(l2 - lm)
            y = (e0 * o_scr[0, j, rows, :] + e1 * o_scr[1, j, rows, :] + e2 * o_scr[2, j, rows, :]) / (e0 + e1 + e2)
            yb_ref[0, rows, j * LANES:(j + 1) * LANES] = y.astype(yb_ref.dtype)
        return carry

    lax.fori_loop(0, s // MERGE_ROWS, merge_body, 0)


def _dilated_attention(qkv, bias):
    bn, s, _ = qkv[0].shape
    in_specs = []
    for d in B_DILATIONS:
        if d == 1:
            spec = pl.BlockSpec((1, s, B_GROUP_WIDTH), lambda b: (b, 0, 0))
        else:
            spec = pl.BlockSpec((1, d, s // d, B_GROUP_WIDTH), lambda b: (b, 0, 0, 0))
        in_specs += [spec] * 3
    in_specs.append(_const_spec((B_GROUPS, B_HEADS_PER_GROUP * BLOCK, 2 * BLOCK)))
    return pl.pallas_call(
        functools.partial(_attn_kernel, s=s),
        grid=(bn,),
        in_specs=in_specs,
        out_specs=pl.BlockSpec((1, s, B_GROUP_WIDTH), lambda b: (b, 0, 0)),
        out_shape=jax.ShapeDtypeStruct((bn, s, B_GROUP_WIDTH), BF16),
        scratch_shapes=[pltpu.VMEM((B_GROUPS, SLABS, s, LANES), F32),
                        pltpu.VMEM((B_GROUPS, SLABS, s, LANES), F32)],
        compiler_params=pltpu.CompilerParams(
            dimension_semantics=("parallel",), vmem_limit_bytes=VMEM_LIMIT_BYTES),
        name="dilated_attn",
    )(*qkv, bias)


def _memkv_kernel(mem_ref, nw_ref, w_ref, kw_ref, mk_ref, mv_ref):
    rows = mem_ref.shape[0] * MEM_LEN
    hm = (_rms_rows(mem_ref[...].reshape(rows, D_MODEL)) * nw_ref[...]).astype(BF16)
    kv = _dot(hm, w_ref[...])
    for h in range(M_HEADS):
        cols = slice(h * M_HEAD_DIM, (h + 1) * M_HEAD_DIM)
        mk = (_rms_rows(kv[:, cols]) * kw_ref[...]).astype(BF16)
        mk_ref[:, :, cols] = mk.reshape(mem_ref.shape[0], MEM_LEN, M_HEAD_DIM)
    mv_ref[...] = kv[:, M_WIDTH:].astype(BF16).reshape(mv_ref.shape)


def _mem_kv(mem, mem_norm_w, w_kv, k_norm_w):
    bn = mem.shape[0]
    mb = math.gcd(bn, MEMKV_BATCH)
    out = jax.ShapeDtypeStruct((bn, MEM_LEN, M_WIDTH), BF16)
    ospec = pl.BlockSpec((mb, MEM_LEN, M_WIDTH), lambda b: (b, 0, 0))
    return pl.pallas_call(
        _memkv_kernel,
        grid=(bn // mb,),
        in_specs=[
            pl.BlockSpec((mb, MEM_LEN, D_MODEL), lambda b: (b, 0, 0)),
            _const_spec((1, D_MODEL)),
            _const_spec((D_MODEL, 2 * M_WIDTH)),
            _const_spec((1, M_HEAD_DIM)),
        ],
        out_specs=[ospec, ospec],
        out_shape=[out, out],
        compiler_params=pltpu.CompilerParams(dimension_semantics=("parallel",)),
        name="mem_kv",
    )(mem, mem_norm_w, w_kv, k_norm_w)


MW_AU, MW_AV, MW_AZ = 0, 768, 1536
MW_BZ, MW_MQ, MW_MZ = 4608, 4864, 5376
MW_G = 5888
MW_TOTAL = MW_G + 3 * D_MODEL


def _main_kernel(x_ref, yb_ref, mk_ref, mv_ref,
                 nw_ref, w_ref, gb_ref, lnw_ref, lnb_ref, sw_ref, sb_ref, mqw_ref,
                 pa_ref, pb_ref, pm_ref, wo_ref, out_ref, *, tm):
    xf = x_ref[...]
    h = (_rms_rows(xf) * nw_ref[...]).astype(BF16)

    def proj(lo, width):
        return _dot(h, w_ref[:, lo:lo + width])

    def gate(i):
        return jax.nn.sigmoid(proj(MW_G + i * D_MODEL, D_MODEL) + gb_ref[i:i + 1, :])

    u = _gelu_exact(proj(MW_AU, A_WIDTH))
    gv = _gelu_exact(proj(MW_AV, A_WIDTH))
    mu = jnp.mean(gv, axis=-1, keepdims=True)
    gc = gv - mu
    vv = gc * lax.rsqrt(jnp.mean(gc * gc, axis=-1, keepdims=True) + EPS) * lnw_ref[...] + lnb_ref[...]
    vv = vv.astype(BF16)
    ti = lax.broadcasted_iota(jnp.int32, (CHUNK, CHUNK), 0)
    si = lax.broadcasted_iota(jnp.int32, (CHUNK, CHUNK), 1)
    causal = si <= ti
    ws = [jnp.where(causal, sw_ref[g], 0.0).astype(BF16) for g in range(A_GROUPS)]
    win_lo = [(g * A_GROUP_DIM) // LANES * LANES for g in range(A_GROUPS)]
    lane = lax.broadcasted_iota(jnp.int32, (1, LANES), 1)
    mixed_chunks = []
    for c in range(tm // CHUNK):
        vc = vv[c * CHUNK:(c + 1) * CHUNK, :]
        m = [_dot(ws[g], vc[:, win_lo[g]:win_lo[g] + 2 * LANES]) for g in range(A_GROUPS)]
        lo, hi = (lambda t: t[:, :LANES]), (lambda t: t[:, LANES:])
        in_first = lane < (A_GROUP_DIM - LANES)
        pieces = [lo(m[0]), jnp.where(in_first, hi(m[0]), lo(m[1])), hi(m[1]),
                  lo(m[2]), jnp.where(in_first, hi(m[2]), lo(m[3])), hi(m[3])]
        mixed_chunks.append(jnp.concatenate(pieces, axis=-1) + sb_ref[...])
    mixed = jnp.concatenate(mixed_chunks, axis=0)
    y_a = (u * mixed * jax.nn.silu(proj(MW_AZ, A_WIDTH))).astype(BF16)
    acc = gate(0) * _dot(y_a, pa_ref[...])

    mq = proj(MW_MQ, M_WIDTH)
    ym_heads = []
    for hd in range(M_HEADS):
        cols = slice(hd * M_HEAD_DIM, (hd + 1) * M_HEAD_DIM)
        qn = (_rms_rows(mq[:, cols]) * mqw_ref[...]).astype(BF16)
        sc = _dot_nt(qn, mk_ref[0, :, cols]) * (M_HEAD_DIM ** -0.5)
        sc = sc - jnp.max(sc, axis=-1, keepdims=True)
        p = jnp.exp(sc)
        l = jnp.sum(p, axis=-1, keepdims=True)
        ym_heads.append(_dot(p.astype(BF16), mv_ref[0, :, cols]) / l)
    y_m = (jnp.concatenate(ym_heads, axis=-1) * jax.nn.silu(proj(MW_MZ, M_WIDTH))).astype(BF16)
    acc = acc + gate(2) * _dot(y_m, pm_ref[...])

    y_b = (yb_ref[...].astype(F32) * jax.nn.silu(proj(MW_BZ, B_GROUP_WIDTH))).astype(BF16)
    acc = acc + gate(1) * _dot(y_b, pb_ref[...])

    out_ref[...] = xf + _dot(acc.astype(BF16), wo_ref[...])


def _main_block(x2, yb, mk, mv, norm_w, w_main, gate_b, lnw, lnb, sw, sb_full, mqw,
                proj_a, proj_b, proj_m, w_out, s):
    n = x2.shape[0]
    tm = MAIN_TM
    steps_per_batch = s // tm
    tile = lambda width: pl.BlockSpec((tm, width), lambda i: (i, 0))
    mem_spec = pl.BlockSpec((1, MEM_LEN, M_WIDTH), lambda i: (i // steps_per_batch, 0, 0))
    return pl.pallas_call(
        functools.partial(_main_kernel, tm=tm),
        grid=(n // tm,),
        in_specs=[tile(D_MODEL), tile(B_GROUP_WIDTH), mem_spec, mem_spec,
                  _const_spec((1, D_MODEL)),
                  _const_spec((D_MODEL, MW_TOTAL)),
                  _const_spec((3, D_MODEL)),
                  _const_spec((1, A_WIDTH)),
                  _const_spec((1, A_WIDTH)),
                  _const_spec((A_GROUPS, CHUNK, CHUNK)),
                  _const_spec((CHUNK, A_WIDTH)),
                  _const_spec((1, M_HEAD_DIM)),
                  _const_spec((A_WIDTH, D_MODEL)),
                  _const_spec((B_GROUP_WIDTH, D_MODEL)),
                  _const_spec((M_WIDTH, D_MODEL)),
                  _const_spec((D_MODEL, D_MODEL))],
        out_specs=tile(D_MODEL),
        out_shape=jax.ShapeDtypeStruct((n, D_MODEL), F32),
        compiler_params=pltpu.CompilerParams(
            dimension_semantics=("parallel",), vmem_limit_bytes=VMEM_LIMIT_BYTES),
        name="main_block",
    )(x2, yb, mk, mv, norm_w, w_main, gate_b, lnw, lnb, sw, sb_full, mqw,
      proj_a, proj_b, proj_m, w_out)


def kernel(x, mem, norm_w, w_in, gate_b, a_v_norm_w, a_v_norm_b, a_spatial_w, a_spatial_b,
           b_q_norm_w, b_k_norm_w, rel_bias, mem_norm_w, m_w_kv, m_q_norm_w, m_k_norm_w,
           proj_a, proj_b, proj_m, w_out):
    bn, s, _ = x.shape
    n = bn * s
    row = lambda v: v.reshape(1, -1)

    w_bf = w_in.astype(BF16)
    wq_t = row(jnp.tile(b_q_norm_w * (B_HEAD_DIM ** -0.5 * LOG2E), B_HEADS_PER_GROUP))
    wk_t = row(jnp.tile(b_k_norm_w, B_HEADS_PER_GROUP))
    sb_full = jnp.repeat(a_spatial_b.T, A_GROUP_DIM, axis=1)

    bias = _expand_bias(rel_bias)
    qkv = _qkv_proj(x, row(norm_w), w_bf, wq_t, wk_t)
    yb = _dilated_attention(qkv, bias)
    mk, mv = _mem_kv(mem, row(mem_norm_w), m_w_kv.astype(BF16), row(m_k_norm_w))
    out = _main_block(x.reshape(n, D_MODEL), yb.reshape(n, B_GROUP_WIDTH), mk, mv, row(norm_w), w_bf, gate_b,
                      row(a_v_norm_w), row(a_v_norm_b), a_spatial_w, sb_full, row(m_q_norm_w),
                      proj_a.astype(BF16), proj_b.astype(BF16), proj_m.astype(BF16), w_out.astype(BF16), s)
    return out.reshape(bn, s, D_MODEL)
```

```python
import functools
import math

import jax
import jax.numpy as jnp
import numpy as np
from jax import lax
from jax.experimental import pallas as pl
from jax.experimental.pallas import tpu as pltpu

EPS = 1e-6
D_MODEL = 1024
A_WIDTH = 768
A_GROUPS = 4
A_GROUP_DIM = A_WIDTH // A_GROUPS
CHUNK = 128
B_PATTERNS = ((128, 1), (512, 4), (2048, 16))
B_DILATIONS = tuple(d for _, d in B_PATTERNS)
B_GROUPS = 3
B_HEADS_PER_GROUP = 4
B_HEAD_DIM = 64
B_GROUP_WIDTH = B_HEADS_PER_GROUP * B_HEAD_DIM
B_QKV_WIDTH = B_GROUPS * B_GROUP_WIDTH
BLOCK = 128
MEM_LEN = 256
M_HEADS = 4
M_HEAD_DIM = 128
M_WIDTH = M_HEADS * M_HEAD_DIM
REL_BUCKETS = 32
REL_MAX_DISTANCE = 2048
NEG = -1e30
LOG2E = math.log2(math.e)

OFF_BQ = 2304
OFF_BZ = 4608

LANES = 128
SLABS = B_GROUP_WIDTH // LANES
VMEM_LIMIT_BYTES = 56 * 1024 * 1024

QKV_TM = 1024
MAIN_TM = 1024
MERGE_ROWS = 256
MEMKV_BATCH = 4
assert OFF_BQ % (OFF_BZ - OFF_BQ) == 0

BF16 = jnp.bfloat16
F32 = jnp.float32


def _dot(a, b):
    return jnp.dot(a, b, preferred_element_type=F32)


def _dot_nt(a, b):
    return lax.dot_general(a, b, (((1,), (1,)), ((), ())), preferred_element_type=F32)


def _rms_rows(xf):
    return xf * lax.rsqrt(jnp.mean(xf * xf, axis=-1, keepdims=True) + EPS)


def _gelu_exact(t):
    return 0.5 * t * (1.0 + lax.erf(t * (2.0 ** -0.5)))


def _const_spec(shape):
    nd = len(shape)
    return pl.BlockSpec(shape, lambda *_: (0,) * nd, pipeline_mode=pl.Buffered(1))


def _t5_causal_bucket(dist):
    max_exact = REL_BUCKETS // 2
    df = np.maximum(dist, 1).astype(np.float32)
    scaled = (np.log(df / np.float32(max_exact)) / np.float32(math.log(REL_MAX_DISTANCE / max_exact))
              * np.float32(REL_BUCKETS - max_exact))
    large = np.minimum(max_exact + scaled.astype(np.int32), REL_BUCKETS - 1)
    return np.where(dist < max_exact, dist, large).astype(np.int32)


def _bias_kernel(bucket_ref, valid_ref, rb_ref, out_ref):
    g = pl.program_id(0)
    bk = bucket_ref[0]
    valid = valid_ref[...] != 0
    for h in range(B_HEADS_PER_GROUP):
        acc = jnp.zeros((BLOCK, 2 * BLOCK), F32)
        for b in range(REL_BUCKETS):
            acc = jnp.where(bk == b, rb_ref[b, g * B_HEADS_PER_GROUP + h], acc)
        out_ref[0, h * BLOCK:(h + 1) * BLOCK, :] = jnp.where(valid, acc * LOG2E, NEG)


def _expand_bias(rel_bias):
    qi = np.arange(BLOCK, dtype=np.int32)[:, None]
    kj = np.arange(2 * BLOCK, dtype=np.int32)[None, :]
    step = qi + BLOCK - kj
    buckets = np.stack([_t5_causal_bucket(np.maximum(step, 0) * d) for d in B_DILATIONS])
    valid = ((step >= 0) & (step <= BLOCK)).astype(np.int32)
    return pl.pallas_call(
        _bias_kernel,
        grid=(B_GROUPS,),
        in_specs=[
            pl.BlockSpec((1, BLOCK, 2 * BLOCK), lambda g: (g, 0, 0)),
            pl.BlockSpec((BLOCK, 2 * BLOCK), lambda g: (0, 0)),
            pl.BlockSpec(memory_space=pltpu.SMEM),
        ],
        out_specs=pl.BlockSpec((1, B_HEADS_PER_GROUP * BLOCK, 2 * BLOCK), lambda g: (g, 0, 0)),
        out_shape=jax.ShapeDtypeStruct((B_GROUPS, B_HEADS_PER_GROUP * BLOCK, 2 * BLOCK), F32),
        name="rel_bias_expand",
    )(buckets, valid, rel_bias)


def _qkv_kernel(x_ref, nw_ref, w_ref, wq_ref, wk_ref, *refs, tm):
    out_refs = refs[:3 * B_GROUPS]
    scratch = refs[3 * B_GROUPS:]
    h = (_rms_rows(x_ref[0]) * nw_ref[...]).astype(BF16)
    r = lax.broadcasted_iota(jnp.int32, (B_GROUP_WIDTH, B_GROUP_WIDTH), 0) // B_HEAD_DIM
    c = lax.broadcasted_iota(jnp.int32, (B_GROUP_WIDTH, B_GROUP_WIDTH), 1) // B_HEAD_DIM
    seg = jnp.where(r == c, 1.0 / B_HEAD_DIM, 0.0).astype(BF16)

    def head_norm(t, w):
        ms = _dot((t * t).astype(BF16), seg)
        return t * lax.rsqrt(ms + EPS) * w

    def emit(g, which, t):
        out = out_refs[3 * g + which]
        d = B_DILATIONS[g]
        if d == 1:
            out[0] = t.astype(BF16)
            return
        scr = scratch[3 * (g - 1) + which]
        for j in range(SLABS):
            scr[j] = t[:, j * LANES:(j + 1) * LANES]
        for res in range(d):
            for j in range(SLABS):
                out[0, res, :, j * LANES:(j + 1) * LANES] = (
                    scr[j, pl.ds(res, tm // d, stride=d), :].astype(BF16))

    def piece(g, which):
        lo = which * B_QKV_WIDTH + g * B_GROUP_WIDTH
        return _dot(h, w_ref[:, lo:lo + B_GROUP_WIDTH])

    qk = [[piece(g, which) for which in range(2)] for g in range(B_GROUPS)]
    for g in reversed(range(B_GROUPS)):
        emit(g, 0, head_norm(qk[g][0], wq_ref[...]))
        emit(g, 1, head_norm(qk[g][1], wk_ref[...]))
        emit(g, 2, piece(g, 2))


def _qkv_proj(x, norm_w, w_qkv, wq_t, wk_t):
    bn, s, _ = x.shape
    tm = QKV_TM
    out_shapes, out_specs, scratch = [], [], []
    for d in B_DILATIONS:
        if d == 1:
            shape = jax.ShapeDtypeStruct((bn, s, B_GROUP_WIDTH), BF16)
            spec = pl.BlockSpec((1, tm, B_GROUP_WIDTH), lambda b, j: (b, j, 0))
        else:
            shape = jax.ShapeDtypeStruct((bn, d, s // d, B_GROUP_WIDTH), BF16)
            spec = pl.BlockSpec((1, d, tm // d, B_GROUP_WIDTH), lambda b, j: (b, 0, j, 0))
            scratch += [pltpu.VMEM((SLABS, tm, LANES), F32)] * 3
        out_shapes += [shape] * 3
        out_specs += [spec] * 3
    return pl.pallas_call(
        functools.partial(_qkv_kernel, tm=tm),
        grid=(bn, s // tm),
        in_specs=[
            pl.BlockSpec((1, tm, D_MODEL), lambda b, j: (b, j, 0)),
            _const_spec((1, D_MODEL)),
            pl.BlockSpec((D_MODEL, OFF_BZ - OFF_BQ), lambda b, j: (0, OFF_BQ // (OFF_BZ - OFF_BQ)),
                         pipeline_mode=pl.Buffered(1)),
            _const_spec((1, B_GROUP_WIDTH)),
            _const_spec((1, B_GROUP_WIDTH)),
        ],
        out_specs=out_specs,
        out_shape=out_shapes,
        scratch_shapes=scratch,
        compiler_params=pltpu.CompilerParams(
            dimension_semantics=("parallel", "parallel"), vmem_limit_bytes=VMEM_LIMIT_BYTES),
        name="qkv_proj",
    )(x, norm_w, w_qkv, wq_t, wk_t)


def _attn_kernel(q0, k0, v0, q1, k1, v1, q2, k2, v2, bias_ref, yb_ref, o_scr, lse_scr, *, s):
    lane_head = lax.broadcasted_iota(jnp.int32, (1, B_GROUP_WIDTH), 1) // B_HEAD_DIM
    masks = [lane_head == h for h in range(B_HEADS_PER_GROUP)]

    def block(qb, kb, vb, bias):
        qs = jnp.concatenate([jnp.where(m, qb, jnp.zeros_like(qb)) for m in masks], axis=0)
        sc = _dot_nt(qs, kb) + bias
        mx = jnp.max(sc, axis=-1, keepdims=True)
        p = jnp.exp2(sc - mx)
        l = jnp.sum(p, axis=-1, keepdims=True)
        o = _dot(p.astype(BF16), vb) / l
        lse = mx + jnp.log2(l)
        o_out = jnp.zeros((BLOCK, B_GROUP_WIDTH), F32)
        lse_out = jnp.zeros((BLOCK, B_GROUP_WIDTH), F32)
        for h, m in enumerate(masks):
            rows = slice(h * BLOCK, (h + 1) * BLOCK)
            o_out = jnp.where(m, o[rows], o_out)
            lse_out = jnp.where(m, lse[rows], lse_out)
        return o_out, lse_out

    def emit(g, start, res):
        d = B_DILATIONS[g]
        idx = pl.ds(start, BLOCK) if d == 1 else pl.ds(start, BLOCK, stride=d)
        o_out, lse_out = res
        for j in range(SLABS):
            o_scr[g, j, idx, :] = o_out[:, j * LANES:(j + 1) * LANES]
            lse_scr[g, j, idx, :] = lse_out[:, j * LANES:(j + 1) * LANES]

    def first_block(g, q, k, v):
        return block(q[0:BLOCK, :], k[0:BLOCK, :], v[0:BLOCK, :], bias_ref[g, :, BLOCK:2 * BLOCK])

    def later_block(g, q, k, v, n):
        q_lo = pl.multiple_of(n * BLOCK, BLOCK)
        k_lo = pl.multiple_of((n - 1) * BLOCK, BLOCK)
        return block(q[pl.ds(q_lo, BLOCK), :], k[pl.ds(k_lo, 2 * BLOCK), :],
                     v[pl.ds(k_lo, 2 * BLOCK), :], bias_ref[g])

    emit(0, 0, first_block(0, q0.at[0], k0.at[0], v0.at[0]))

    def g0_body(n, carry):
        emit(0, pl.multiple_of(n * BLOCK, BLOCK), later_block(0, q0.at[0], k0.at[0], v0.at[0], n))
        return carry

    lax.fori_loop(1, s // BLOCK, g0_body, 0, unroll=5)

    for g, (q, k, v) in ((1, (q1, k1, v1)), (2, (q2, k2, v2))):
        d = B_DILATIONS[g]
        nb = s // d // BLOCK

        def res_body(res, carry, g=g, q=q, k=k, v=v, d=d, nb=nb):
            qr, kr, vr = q.at[0, res], k.at[0, res], v.at[0, res]
            emit(g, res, first_block(g, qr, kr, vr))
            if nb > 1:
                def blk_body(n, c2):
                    emit(g, res + n * (BLOCK * d), later_block(g, qr, kr, vr, n))
                    return c2

                lax.fori_loop(1, nb, blk_body, 0, unroll=True)
            return carry

        lax.fori_loop(0, d, res_body, 0, unroll=2 if nb > 1 else 8)

    def merge_body(c, carry):
        rows = pl.ds(pl.multiple_of(c * MERGE_ROWS, MERGE_ROWS), MERGE_ROWS)
        for j in range(SLABS):
            l0, l1, l2 = lse_scr[0, j, rows, :], lse_scr[1, j, rows, :], lse_scr[2, j, rows, :]
            lm = jnp.maximum(jnp.maximum(l0, l1), l2)
            e0, e1, e2 = jnp.exp2(l0 - lm), jnp.exp2(l1 - lm), jnp.exp2(l2 - lm)
            y = (e0 * o_scr[0, j, rows, :] + e1 * o_scr[1, j, rows, :] + e2 * o_scr[2, j, rows, :]) / (e0 + e1 + e2)
            yb_ref[0, rows, j * LANES:(j + 1) * LANES] = y.astype(yb_ref.dtype)
        return carry

    lax.fori_loop(0, s // MERGE_ROWS, merge_body, 0)


def _dilated_attention(qkv, bias):
    bn, s, _ = qkv[0].shape
    in_specs = []
    for d in B_DILATIONS:
        if d == 1:
            spec = pl.BlockSpec((1, s, B_GROUP_WIDTH), lambda b: (b, 0, 0))
        else:
            spec = pl.BlockSpec((1, d, s // d, B_GROUP_WIDTH), lambda b: (b, 0, 0, 0))
        in_specs += [spec] * 3
    in_specs.append(_const_spec((B_GROUPS, B_HEADS_PER_GROUP * BLOCK, 2 * BLOCK)))
    return pl.pallas_call(
        functools.partial(_attn_kernel, s=s),
        grid=(bn,),
        in_specs=in_specs,
        out_specs=pl.BlockSpec((1, s, B_GROUP_WIDTH), lambda b: (b, 0, 0)),
        out_shape=jax.ShapeDtypeStruct((bn, s, B_GROUP_WIDTH), BF16),
        scratch_shapes=[pltpu.VMEM((B_GROUPS, SLABS, s, LANES), F32),
                        pltpu.VMEM((B_GROUPS, SLABS, s, LANES), F32)],
        compiler_params=pltpu.CompilerParams(
            dimension_semantics=("parallel",), vmem_limit_bytes=VMEM_LIMIT_BYTES),
        name="dilated_attn",
    )(*qkv, bias)


def _memkv_kernel(mem_ref, nw_ref, w_ref, kw_ref, mk_ref, mv_ref):
    rows = mem_ref.shape[0] * MEM_LEN
    hm = (_rms_rows(mem_ref[...].reshape(rows, D_MODEL)) * nw_ref[...]).astype(BF16)
    kv = _dot(hm, w_ref[...])
    for h in range(M_HEADS):
        cols = slice(h * M_HEAD_DIM, (h + 1) * M_HEAD_DIM)
        mk = (_rms_rows(kv[:, cols]) * kw_ref[...]).astype(BF16)
        mk_ref[:, :, cols] = mk.reshape(mem_ref.shape[0], MEM_LEN, M_HEAD_DIM)
    mv_ref[...] = kv[:, M_WIDTH:].astype(BF16).reshape(mv_ref.shape)


def _mem_kv(mem, mem_norm_w, w_kv, k_norm_w):
    bn = mem.shape[0]
    mb = math.gcd(bn, MEMKV_BATCH)
    out = jax.ShapeDtypeStruct((bn, MEM_LEN, M_WIDTH), BF16)
    ospec = pl.BlockSpec((mb, MEM_LEN, M_WIDTH), lambda b: (b, 0, 0))
    return pl.pallas_call(
        _memkv_kernel,
        grid=(bn // mb,),
        in_specs=[
            pl.BlockSpec((mb, MEM_LEN, D_MODEL), lambda b: (b, 0, 0)),
            _const_spec((1, D_MODEL)),
            _const_spec((D_MODEL, 2 * M_WIDTH)),
            _const_spec((1, M_HEAD_DIM)),
        ],
        out_specs=[ospec, ospec],
        out_shape=[out, out],
        compiler_params=pltpu.CompilerParams(dimension_semantics=("parallel",)),
        name="mem_kv",
    )(mem, mem_norm_w, w_kv, k_norm_w)


MW_AU, MW_AV, MW_AZ = 0, 768, 1536
MW_BZ, MW_MQ, MW_MZ = 4608, 4864, 5376
MW_G = 5888
MW_TOTAL = MW_G + 3 * D_MODEL


def _main_kernel(x_ref, yb_ref, mk_ref, mv_ref,
                 nw_ref, w_ref, gb_ref, lnw_ref, lnb_ref, sw_ref, sb_ref, mqw_ref,
                 pa_ref, pb_ref, pm_ref, wo_ref, out_ref, *, tm):
    xf = x_ref[...]
    h = (_rms_rows(xf) * nw_ref[...]).astype(BF16)

    def proj(lo, width):
        return _dot(h, w_ref[:, lo:lo + width])

    def gate(i):
        return jax.nn.sigmoid(proj(MW_G + i * D_MODEL, D_MODEL) + gb_ref[i:i + 1, :])

    u = _gelu_exact(proj(MW_AU, A_WIDTH))
    gv = _gelu_exact(proj(MW_AV, A_WIDTH))
    mu = jnp.mean(gv, axis=-1, keepdims=True)
    gc = gv - mu
    vv = gc * lax.rsqrt(jnp.mean(gc * gc, axis=-1, keepdims=True) + EPS) * lnw_ref[...] + lnb_ref[...]
    vv = vv.astype(BF16)
    ti = lax.broadcasted_iota(jnp.int32, (CHUNK, CHUNK), 0)
    si = lax.broadcasted_iota(jnp.int32, (CHUNK, CHUNK), 1)
    causal = si <= ti
    ws = [jnp.where(causal, sw_ref[g], 0.0).astype(BF16) for g in range(A_GROUPS)]
    win_lo = [(g * A_GROUP_DIM) // LANES * LANES for g in range(A_GROUPS)]
    lane = lax.broadcasted_iota(jnp.int32, (1, LANES), 1)
    mixed_chunks = []
    for c in range(tm // CHUNK):
        vc = vv[c * CHUNK:(c + 1) * CHUNK, :]
        m = [_dot(ws[g], vc[:, win_lo[g]:win_lo[g] + 2 * LANES]) for g in range(A_GROUPS)]
        lo, hi = (lambda t: t[:, :LANES]), (lambda t: t[:, LANES:])
        in_first = lane < (A_GROUP_DIM - LANES)
        pieces = [lo(m[0]), jnp.where(in_first, hi(m[0]), lo(m[1])), hi(m[1]),
                  lo(m[2]), jnp.where(in_first, hi(m[2]), lo(m[3])), hi(m[3])]
        mixed_chunks.append(jnp.concatenate(pieces, axis=-1) + sb_ref[...])
    mixed = jnp.concatenate(mixed_chunks, axis=0)
    y_a = (u * mixed * jax.nn.silu(proj(MW_AZ, A_WIDTH))).astype(BF16)
    acc = gate(0) * _dot(y_a, pa_ref[...])

    mq = proj(MW_MQ, M_WIDTH)
    ym_heads = []
    for hd in range(M_HEADS):
        cols = slice(hd * M_HEAD_DIM, (hd + 1) * M_HEAD_DIM)
        qn = (_rms_rows(mq[:, cols]) * mqw_ref[...]).astype(BF16)
        sc = _dot_nt(qn, mk_ref[0, :, cols]) * (M_HEAD_DIM ** -0.5)
        sc = sc - jnp.max(sc, axis=-1, keepdims=True)
        p = jnp.exp(sc)
        l = jnp.sum(p, axis=-1, keepdims=True)
        ym_heads.append(_dot(p.astype(BF16), mv_ref[0, :, cols]) / l)
    y_m = (jnp.concatenate(ym_heads, axis=-1) * jax.nn.silu(proj(MW_MZ, M_WIDTH))).astype(BF16)
    acc = acc + gate(2) * _dot(y_m, pm_ref[...])

    y_b = (yb_ref[...].astype(F32) * jax.nn.silu(proj(MW_BZ, B_GROUP_WIDTH))).astype(BF16)
    acc = acc + gate(1) * _dot(y_b, pb_ref[...])

    out_ref[...] = xf + _dot(acc.astype(BF16), wo_ref[...])


def _main_block(x2, yb, mk, mv, norm_w, w_main, gate_b, lnw, lnb, sw, sb_full, mqw,
                proj_a, proj_b, proj_m, w_out, s):
    n = x2.shape[0]
    tm = MAIN_TM
    steps_per_batch = s // tm
    tile = lambda width: pl.BlockSpec((tm, width), lambda i: (i, 0))
    mem_spec = pl.BlockSpec((1, MEM_LEN, M_WIDTH), lambda i: (i // steps_per_batch, 0, 0))
    return pl.pallas_call(
        functools.partial(_main_kernel, tm=tm),
        grid=(n // tm,),
        in_specs=[tile(D_MODEL), tile(B_GROUP_WIDTH), mem_spec, mem_spec,
                  _const_spec((1, D_MODEL)),
                  _const_spec((D_MODEL, MW_TOTAL)),
                  _const_spec((3, D_MODEL)),
                  _const_spec((1, A_WIDTH)),
                  _const_spec((1, A_WIDTH)),
                  _const_spec((A_GROUPS, CHUNK, CHUNK)),
                  _const_spec((CHUNK, A_WIDTH)),
                  _const_spec((1, M_HEAD_DIM)),
                  _const_spec((A_WIDTH, D_MODEL)),
                  _const_spec((B_GROUP_WIDTH, D_MODEL)),
                  _const_spec((M_WIDTH, D_MODEL)),
                  _const_spec((D_MODEL, D_MODEL))],
        out_specs=tile(D_MODEL),
        out_shape=jax.ShapeDtypeStruct((n, D_MODEL), F32),
        compiler_params=pltpu.CompilerParams(
            dimension_semantics=("parallel",), vmem_limit_bytes=VMEM_LIMIT_BYTES),
        name="main_block",
    )(x2, yb, mk, mv, norm_w, w_main, gate_b, lnw, lnb, sw, sb_full, mqw,
      proj_a, proj_b, proj_m, w_out)


def kernel(x, mem, norm_w, w_in, gate_b, a_v_norm_w, a_v_norm_b, a_spatial_w, a_spatial_b,
           b_q_norm_w, b_k_norm_w, rel_bias, mem_norm_w, m_w_kv, m_q_norm_w, m_k_norm_w,
           proj_a, proj_b, proj_m, w_out):
    bn, s, _ = x.shape
    n = bn * s
    row = lambda v: v.reshape(1, -1)

    w_bf = w_in.astype(BF16)
    wq_t = row(jnp.tile(b_q_norm_w * (B_HEAD_DIM ** -0.5 * LOG2E), B_HEADS_PER_GROUP))
    wk_t = row(jnp.tile(b_k_norm_w, B_HEADS_PER_GROUP))
    sb_full = jnp.repeat(a_spatial_b.T, A_GROUP_DIM, axis=1)

    bias = _expand_bias(rel_bias)
    qkv = _qkv_proj(x, row(norm_w), w_bf, wq_t, wk_t)
    yb = _dilated_attention(qkv, bias)
    mk, mv = _mem_kv(mem, row(mem_norm_w), m_w_kv.astype(BF16), row(m_k_norm_w))
    out = _main_block(x.reshape(n, D_MODEL), yb.reshape(n, B_GROUP_WIDTH), mk, mv, row(norm_w), w_bf, gate_b,
                      row(a_v_norm_w), row(a_v_norm_b), a_spatial_w, sb_full, row(m_q_norm_w),
                      proj_a.astype(BF16), proj_b.astype(BF16), proj_m.astype(BF16), w_out.astype(BF16), s)
    return out.reshape(bn, s, D_MODEL)
```

```python
import functools
import math

import jax
import jax.numpy as jnp
import numpy as np
from jax import lax
from jax.experimental import pallas as pl
from jax.experimental.pallas import tpu as pltpu

EPS = 1e-6
D_MODEL = 1024
A_WIDTH = 768
A_GROUPS = 4
A_GROUP_DIM = A_WIDTH // A_GROUPS
CHUNK = 128
B_PATTERNS = ((128, 1), (512, 4), (2048, 16))
B_DILATIONS = tuple(d for _, d in B_PATTERNS)
B_GROUPS = 3
B_HEADS_PER_GROUP = 4
B_HEAD_DIM = 64
B_GROUP_WIDTH = B_HEADS_PER_GROUP * B_HEAD_DIM
B_QKV_WIDTH = B_GROUPS * B_GROUP_WIDTH
BLOCK = 128
MEM_LEN = 256
M_HEADS = 4
M_HEAD_DIM = 128
M_WIDTH = M_HEADS * M_HEAD_DIM
REL_BUCKETS = 32
REL_MAX_DISTANCE = 2048
NEG = -1e30
LOG2E = math.log2(math.e)

OFF_BQ = 2304
OFF_BZ = 4608

LANES = 128
SLABS = B_GROUP_WIDTH // LANES
VMEM_LIMIT_BYTES = 60 * 1024 * 1024

QKV_TM = 1024
MAIN_TM = 1024
MERGE_ROWS = 256
MEMKV_BATCH = 4
MAIN_SUBTILES = 2
assert OFF_BQ % (OFF_BZ - OFF_BQ) == 0

BF16 = jnp.bfloat16
F32 = jnp.float32


def _dot(a, b):
    return jnp.dot(a, b, preferred_element_type=F32)


def _dot_nt(a, b):
    return lax.dot_general(a, b, (((1,), (1,)), ((), ())), preferred_element_type=F32)


def _rms_rows(xf):
    return xf * lax.rsqrt(jnp.mean(xf * xf, axis=-1, keepdims=True) + EPS)


def _gelu_exact(t):
    return 0.5 * t * (1.0 + lax.erf(t * (2.0 ** -0.5)))


def _const_spec(shape):
    nd = len(shape)
    return pl.BlockSpec(shape, lambda *_: (0,) * nd, pipeline_mode=pl.Buffered(1))


def _t5_causal_bucket(dist):
    max_exact = REL_BUCKETS // 2
    df = np.maximum(dist, 1).astype(np.float32)
    scaled = (np.log(df / np.float32(max_exact)) / np.float32(math.log(REL_MAX_DISTANCE / max_exact))
              * np.float32(REL_BUCKETS - max_exact))
    large = np.minimum(max_exact + scaled.astype(np.int32), REL_BUCKETS - 1)
    return np.where(dist < max_exact, dist, large).astype(np.int32)


def _bias_kernel(bucket_ref, valid_ref, rb_ref, out_ref):
    g = pl.program_id(0)
    bk = bucket_ref[0]
    valid = valid_ref[...] != 0
    for h in range(B_HEADS_PER_GROUP):
        acc = jnp.zeros((BLOCK, 2 * BLOCK), F32)
        for b in range(REL_BUCKETS):
            acc = jnp.where(bk == b, rb_ref[b, g * B_HEADS_PER_GROUP + h], acc)
        out_ref[0, h * BLOCK:(h + 1) * BLOCK, :] = jnp.where(valid, acc * LOG2E, NEG)


def _expand_bias(rel_bias):
    qi = np.arange(BLOCK, dtype=np.int32)[:, None]
    kj = np.arange(2 * BLOCK, dtype=np.int32)[None, :]
    step = qi + BLOCK - kj
    buckets = np.stack([_t5_causal_bucket(np.maximum(step, 0) * d) for d in B_DILATIONS])
    valid = ((step >= 0) & (step <= BLOCK)).astype(np.int32)
    return pl.pallas_call(
        _bias_kernel,
        grid=(B_GROUPS,),
        in_specs=[
            pl.BlockSpec((1, BLOCK, 2 * BLOCK), lambda g: (g, 0, 0)),
            pl.BlockSpec((BLOCK, 2 * BLOCK), lambda g: (0, 0)),
            pl.BlockSpec(memory_space=pltpu.SMEM),
        ],
        out_specs=pl.BlockSpec((1, B_HEADS_PER_GROUP * BLOCK, 2 * BLOCK), lambda g: (g, 0, 0)),
        out_shape=jax.ShapeDtypeStruct((B_GROUPS, B_HEADS_PER_GROUP * BLOCK, 2 * BLOCK), F32),
        name="rel_bias_expand",
    )(buckets, valid, rel_bias)


def _qkv_kernel(x_ref, nw_ref, w_ref, wq_ref, wk_ref, *refs, tm):
    out_refs = refs[:3 * B_GROUPS]
    scratch = refs[3 * B_GROUPS:]
    h = (_rms_rows(x_ref[0]) * nw_ref[...]).astype(BF16)
    r = lax.broadcasted_iota(jnp.int32, (B_GROUP_WIDTH, B_GROUP_WIDTH), 0) // B_HEAD_DIM
    c = lax.broadcasted_iota(jnp.int32, (B_GROUP_WIDTH, B_GROUP_WIDTH), 1) // B_HEAD_DIM
    seg = jnp.where(r == c, 1.0 / B_HEAD_DIM, 0.0).astype(BF16)

    def head_norm(t, w):
        ms = _dot((t * t).astype(BF16), seg)
        return t * lax.rsqrt(ms + EPS) * w

    def emit(g, which, t):
        out = out_refs[3 * g + which]
        d = B_DILATIONS[g]
        if d == 1:
            out[0] = t.astype(BF16)
            return
        scr = scratch[3 * (g - 1) + which]
        for j in range(SLABS):
            scr[j] = t[:, j * LANES:(j + 1) * LANES]
        for res in range(d):
            for j in range(SLABS):
                out[0, res, :, j * LANES:(j + 1) * LANES] = (
                    scr[j, pl.ds(res, tm // d, stride=d), :].astype(BF16))

    def piece(g, which):
        lo = which * B_QKV_WIDTH + g * B_GROUP_WIDTH
        return _dot(h, w_ref[:, lo:lo + B_GROUP_WIDTH])

    qk = [[piece(g, which) for which in range(2)] for g in range(B_GROUPS)]
    for g in reversed(range(B_GROUPS)):
        emit(g, 0, head_norm(qk[g][0], wq_ref[...]))
        emit(g, 1, head_norm(qk[g][1], wk_ref[...]))
        emit(g, 2, piece(g, 2))


def _qkv_proj(x, norm_w, w_qkv, wq_t, wk_t):
    bn, s, _ = x.shape
    tm = QKV_TM
    out_shapes, out_specs, scratch = [], [], []
    for d in B_DILATIONS:
        if d == 1:
            shape = jax.ShapeDtypeStruct((bn, s, B_GROUP_WIDTH), BF16)
            spec = pl.BlockSpec((1, tm, B_GROUP_WIDTH), lambda b, j: (b, j, 0))
        else:
            shape = jax.ShapeDtypeStruct((bn, d, s // d, B_GROUP_WIDTH), BF16)
            spec = pl.BlockSpec((1, d, tm // d, B_GROUP_WIDTH), lambda b, j: (b, 0, j, 0))
            scratch += [pltpu.VMEM((SLABS, tm, LANES), F32)] * 3
        out_shapes += [shape] * 3
        out_specs += [spec] * 3
    return pl.pallas_call(
        functools.partial(_qkv_kernel, tm=tm),
        grid=(bn, s // tm),
        in_specs=[
            pl.BlockSpec((1, tm, D_MODEL), lambda b, j: (b, j, 0)),
            _const_spec((1, D_MODEL)),
            pl.BlockSpec((D_MODEL, OFF_BZ - OFF_BQ), lambda b, j: (0, OFF_BQ // (OFF_BZ - OFF_BQ)),
                         pipeline_mode=pl.Buffered(1)),
            _const_spec((1, B_GROUP_WIDTH)),
            _const_spec((1, B_GROUP_WIDTH)),
        ],
        out_specs=out_specs,
        out_shape=out_shapes,
        scratch_shapes=scratch,
        compiler_params=pltpu.CompilerParams(
            dimension_semantics=("parallel", "parallel"), vmem_limit_bytes=VMEM_LIMIT_BYTES),
        name="qkv_proj",
    )(x, norm_w, w_qkv, wq_t, wk_t)


def _attn_kernel(q0, k0, v0, q1, k1, v1, q2, k2, v2, bias_ref, yb_ref, o_scr, lse_scr, *, s):
    lane_head = lax.broadcasted_iota(jnp.int32, (1, B_GROUP_WIDTH), 1) // B_HEAD_DIM
    masks = [lane_head == h for h in range(B_HEADS_PER_GROUP)]

    def block(qb, kb, vb, bias):
        qs = jnp.concatenate([jnp.where(m, qb, jnp.zeros_like(qb)) for m in masks], axis=0)
        sc = _dot_nt(qs, kb) + bias
        mx = jnp.max(sc, axis=-1, keepdims=True)
        p = jnp.exp2(sc - mx)
        l = jnp.sum(p, axis=-1, keepdims=True)
        o = _dot(p.astype(BF16), vb) / l
        lse = mx + jnp.log2(l)
        o_out = jnp.zeros((BLOCK, B_GROUP_WIDTH), F32)
        lse_out = jnp.zeros((BLOCK, B_GROUP_WIDTH), F32)
        for h, m in enumerate(masks):
            rows = slice(h * BLOCK, (h + 1) * BLOCK)
            o_out = jnp.where(m, o[rows], o_out)
            lse_out = jnp.where(m, lse[rows], lse_out)
        return o_out, lse_out

    def emit(g, start, res):
        d = B_DILATIONS[g]
        idx = pl.ds(start, BLOCK) if d == 1 else pl.ds(start, BLOCK, stride=d)
        o_out, lse_out = res
        for j in range(SLABS):
            o_scr[g, j, idx, :] = o_out[:, j * LANES:(j + 1) * LANES]
            lse_scr[g, j, idx, :] = lse_out[:, j * LANES:(j + 1) * LANES]

    def first_block(g, q, k, v):
        return block(q[0:BLOCK, :], k[0:BLOCK, :], v[0:BLOCK, :], bias_ref[g, :, BLOCK:2 * BLOCK])

    def later_block(g, q, k, v, n):
        q_lo = pl.multiple_of(n * BLOCK, BLOCK)
        k_lo = pl.multiple_of((n - 1) * BLOCK, BLOCK)
        return block(q[pl.ds(q_lo, BLOCK), :], k[pl.ds(k_lo, 2 * BLOCK), :],
                     v[pl.ds(k_lo, 2 * BLOCK), :], bias_ref[g])

    emit(0, 0, first_block(0, q0.at[0], k0.at[0], v0.at[0]))

    def g0_body(n, carry):
        emit(0, pl.multiple_of(n * BLOCK, BLOCK), later_block(0, q0.at[0], k0.at[0], v0.at[0], n))
        return carry

    lax.fori_loop(1, s // BLOCK, g0_body, 0, unroll=5)

    for g, (q, k, v) in ((1, (q1, k1, v1)), (2, (q2, k2, v2))):
        d = B_DILATIONS[g]
        nb = s // d // BLOCK

        def res_body(res, carry, g=g, q=q, k=k, v=v, d=d, nb=nb):
            qr, kr, vr = q.at[0, res], k.at[0, res], v.at[0, res]
            emit(g, res, first_block(g, qr, kr, vr))
            if nb > 1:
                def blk_body(n, c2):
                    emit(g, res + n * (BLOCK * d), later_block(g, qr, kr, vr, n))
                    return c2

                lax.fori_loop(1, nb, blk_body, 0, unroll=True)
            return carry

        lax.fori_loop(0, d, res_body, 0, unroll=2 if nb > 1 else 8)

    def merge_body(c, carry):
        rows = pl.ds(pl.multiple_of(c * MERGE_ROWS, MERGE_ROWS), MERGE_ROWS)
        for j in range(SLABS):
            l0, l1, l2 = lse_scr[0, j, rows, :], lse_scr[1, j, rows, :], lse_scr[2, j, rows, :]
            lm = jnp.maximum(jnp.maximum(l0, l1), l2)
            e0, e1, e2 = jnp.exp2(l0 - lm), jnp.exp2(l1 - lm), jnp.exp2(l2 - lm)
            y = (e0 * o_scr[0, j, rows, :] + e1 * o_scr[1, j, rows, :] + e2 * o_scr[2, j, rows, :]) / (e0 + e1 + e2)
            yb_ref[0, rows, j * LANES:(j + 1) * LANES] = y.astype(yb_ref.dtype)
        return carry

    lax.fori_loop(0, s // MERGE_ROWS, merge_body, 0)


def _dilated_attention(qkv, bias):
    bn, s, _ = qkv[0].shape
    in_specs = []
    for d in B_DILATIONS:
        if d == 1:
            spec = pl.BlockSpec((1, s, B_GROUP_WIDTH), lambda b: (b, 0, 0))
        else:
            spec = pl.BlockSpec((1, d, s // d, B_GROUP_WIDTH), lambda b: (b, 0, 0, 0))
        in_specs += [spec] * 3
    in_specs.append(_const_spec((B_GROUPS, B_HEADS_PER_GROUP * BLOCK, 2 * BLOCK)))
    return pl.pallas_call(
        functools.partial(_attn_kernel, s=s),
        grid=(bn,),
        in_specs=in_specs,
        out_specs=pl.BlockSpec((1, s, B_GROUP_WIDTH), lambda b: (b, 0, 0)),
        out_shape=jax.ShapeDtypeStruct((bn, s, B_GROUP_WIDTH), BF16),
        scratch_shapes=[pltpu.VMEM((B_GROUPS, SLABS, s, LANES), F32),
                        pltpu.VMEM((B_GROUPS, SLABS, s, LANES), F32)],
        compiler_params=pltpu.CompilerParams(
            dimension_semantics=("parallel",), vmem_limit_bytes=VMEM_LIMIT_BYTES),
        name="dilated_attn",
    )(*qkv, bias)


def _memkv_kernel(mem_ref, nw_ref, w_ref, kw_ref, mk_ref, mv_ref):
    rows = mem_ref.shape[0] * MEM_LEN
    hm = (_rms_rows(mem_ref[...].reshape(rows, D_MODEL)) * nw_ref[...]).astype(BF16)
    kv = _dot(hm, w_ref[...])
    for h in range(M_HEADS):
        cols = slice(h * M_HEAD_DIM, (h + 1) * M_HEAD_DIM)
        mk = (_rms_rows(kv[:, cols]) * kw_ref[...]).astype(BF16)
        mk_ref[:, :, cols] = mk.reshape(mem_ref.shape[0], MEM_LEN, M_HEAD_DIM)
    mv_ref[...] = kv[:, M_WIDTH:].astype(BF16).reshape(mv_ref.shape)


def _mem_kv(mem, mem_norm_w, w_kv, k_norm_w):
    bn = mem.shape[0]
    mb = math.gcd(bn, MEMKV_BATCH)
    out = jax.ShapeDtypeStruct((bn, MEM_LEN, M_WIDTH), BF16)
    ospec = pl.BlockSpec((mb, MEM_LEN, M_WIDTH), lambda b: (b, 0, 0))
    return pl.pallas_call(
        _memkv_kernel,
        grid=(bn // mb,),
        in_specs=[
            pl.BlockSpec((mb, MEM_LEN, D_MODEL), lambda b: (b, 0, 0)),
            _const_spec((1, D_MODEL)),
            _const_spec((D_MODEL, 2 * M_WIDTH)),
            _const_spec((1, M_HEAD_DIM)),
        ],
        out_specs=[ospec, ospec],
        out_shape=[out, out],
        compiler_params=pltpu.CompilerParams(dimension_semantics=("parallel",)),
        name="mem_kv",
    )(mem, mem_norm_w, w_kv, k_norm_w)


MW_AU, MW_AV, MW_AZ = 0, 768, 1536
MW_BZ, MW_MQ, MW_MZ = 4608, 4864, 5376
MW_G = 5888
MW_TOTAL = MW_G + 3 * D_MODEL


def _main_kernel(x_ref, yb_ref, mk_ref, mv_ref,
                 nw_ref, w_ref, gb_ref, lnw_ref, lnb_ref, sw_ref, sb_ref, mqw_ref,
                 pa_ref, pb_ref, pm_ref, wo_ref, out_ref, *, tm):
    ti = lax.broadcasted_iota(jnp.int32, (CHUNK, CHUNK), 0)
    si = lax.broadcasted_iota(jnp.int32, (CHUNK, CHUNK), 1)
    ws = [jnp.where(si <= ti, sw_ref[g], 0.0).astype(BF16) for g in range(A_GROUPS)]
    win_lo = [(g * A_GROUP_DIM) // LANES * LANES for g in range(A_GROUPS)]
    lane = lax.broadcasted_iota(jnp.int32, (1, LANES), 1)
    in_first = lane < (A_GROUP_DIM - LANES)

    def sub_tile(rows):
        n_rows = rows.stop - rows.start
        xf = x_ref[rows, :]
        h = (_rms_rows(xf) * nw_ref[...]).astype(BF16)

        def proj(lo, width):
            return _dot(h, w_ref[:, lo:lo + width])

        def gate(i):
            return jax.nn.sigmoid(proj(MW_G + i * D_MODEL, D_MODEL) + gb_ref[i:i + 1, :])

        def mem_head(mq, hd):
            cols = slice(hd * M_HEAD_DIM, (hd + 1) * M_HEAD_DIM)
            qn = (_rms_rows(mq[:, cols]) * mqw_ref[...]).astype(BF16)
            sc = _dot_nt(qn, mk_ref[0, :, cols]) * (M_HEAD_DIM ** -0.5)
            sc = sc - jnp.max(sc, axis=-1, keepdims=True)
            p = jnp.exp(sc)
            l = jnp.sum(p, axis=-1, keepdims=True)
            return _dot(p.astype(BF16), mv_ref[0, :, cols]) / l

        a_u = proj(MW_AU, A_WIDTH)
        a_v = proj(MW_AV, A_WIDTH)
        yield
        u = _gelu_exact(a_u)
        gv = _gelu_exact(a_v)
        yield
        mq = proj(MW_MQ, M_WIDTH)
        a_z = proj(MW_AZ, A_WIDTH)
        yield
        mu = jnp.mean(gv, axis=-1, keepdims=True)
        gc = gv - mu
        vv = gc * lax.rsqrt(jnp.mean(gc * gc, axis=-1, keepdims=True) + EPS) * lnw_ref[...] + lnb_ref[...]
        vv = vv.astype(BF16)
        mixed_chunks = []
        for c in range(n_rows // CHUNK):
            vc = vv[c * CHUNK:(c + 1) * CHUNK, :]
            m = [_dot(ws[g], vc[:, win_lo[g]:win_lo[g] + 2 * LANES]) for g in range(A_GROUPS)]
            lo, hi = (lambda t: t[:, :LANES]), (lambda t: t[:, LANES:])
            pieces = [lo(m[0]), jnp.where(in_first, hi(m[0]), lo(m[1])), hi(m[1]),
                      lo(m[2]), jnp.where(in_first, hi(m[2]), lo(m[3])), hi(m[3])]
            mixed_chunks.append(jnp.concatenate(pieces, axis=-1) + sb_ref[...])
        mixed = jnp.concatenate(mixed_chunks, axis=0)
        y_a = (u * mixed * jax.nn.silu(a_z)).astype(BF16)
        yield
        acc = gate(0) * _dot(y_a, pa_ref[...])
        yield

        ym_heads = [mem_head(mq, hd) for hd in range(M_HEADS)]
        yield
        m_z = proj(MW_MZ, M_WIDTH)
        b_z = proj(MW_BZ, B_GROUP_WIDTH)
        yield
        y_m = (jnp.concatenate(ym_heads, axis=-1) * jax.nn.silu(m_z)).astype(BF16)
        y_b = (yb_ref[rows, :].astype(F32) * jax.nn.silu(b_z)).astype(BF16)
        yield
        acc = acc + gate(2) * _dot(y_m, pm_ref[...])
        yield
        acc = acc + gate(1) * _dot(y_b, pb_ref[...])
        yield
        out_ref[rows, :] = xf + _dot(acc.astype(BF16), wo_ref[...])

    rows_per = tm // MAIN_SUBTILES
    queue = [sub_tile(slice(k * rows_per, (k + 1) * rows_per)) for k in range(MAIN_SUBTILES)]
    done = object()
    live = []
    while queue or live:
        if queue:
            live.append(queue.pop(0))
        live = [gen for gen in live if next(gen, done) is not done]


def _main_block(x2, yb, mk, mv, norm_w, w_main, gate_b, lnw, lnb, sw, sb_full, mqw,
                proj_a, proj_b, proj_m, w_out, s):
    n = x2.shape[0]
    tm = MAIN_TM
    steps_per_batch = s // tm
    tile = lambda width: pl.BlockSpec((tm, width), lambda i: (i, 0))
    mem_spec = pl.BlockSpec((1, MEM_LEN, M_WIDTH), lambda i: (i // steps_per_batch, 0, 0))
    return pl.pallas_call(
        functools.partial(_main_kernel, tm=tm),
        grid=(n // tm,),
        in_specs=[tile(D_MODEL), tile(B_GROUP_WIDTH), mem_spec, mem_spec,
                  _const_spec((1, D_MODEL)),
                  _const_spec((D_MODEL, MW_TOTAL)),
                  _const_spec((3, D_MODEL)),
                  _const_spec((1, A_WIDTH)),
                  _const_spec((1, A_WIDTH)),
                  _const_spec((A_GROUPS, CHUNK, CHUNK)),
                  _const_spec((CHUNK, A_WIDTH)),
                  _const_spec((1, M_HEAD_DIM)),
                  _const_spec((A_WIDTH, D_MODEL)),
                  _const_spec((B_GROUP_WIDTH, D_MODEL)),
                  _const_spec((M_WIDTH, D_MODEL)),
                  _const_spec((D_MODEL, D_MODEL))],
        out_specs=tile(D_MODEL),
        out_shape=jax.ShapeDtypeStruct((n, D_MODEL), F32),
        compiler_params=pltpu.CompilerParams(
            dimension_semantics=("parallel",), vmem_limit_bytes=VMEM_LIMIT_BYTES),
        name="main_block",
    )(x2, yb, mk, mv, norm_w, w_main, gate_b, lnw, lnb, sw, sb_full, mqw,
      proj_a, proj_b, proj_m, w_out)


def kernel(x, mem, norm_w, w_in, gate_b, a_v_norm_w, a_v_norm_b, a_spatial_w, a_spatial_b,
           b_q_norm_w, b_k_norm_w, rel_bias, mem_norm_w, m_w_kv, m_q_norm_w, m_k_norm_w,
           proj_a, proj_b, proj_m, w_out):
    bn, s, _ = x.shape
    n = bn * s
    row = lambda v: v.reshape(1, -1)

    w_bf = w_in.astype(BF16)
    wq_t = row(jnp.tile(b_q_norm_w * (B_HEAD_DIM ** -0.5 * LOG2E), B_HEADS_PER_GROUP))
    wk_t = row(jnp.tile(b_k_norm_w, B_HEADS_PER_GROUP))
    sb_full = jnp.repeat(a_spatial_b.T, A_GROUP_DIM, axis=1)

    bias = _expand_bias(rel_bias)
    qkv = _qkv_proj(x, row(norm_w), w_bf, wq_t, wk_t)
    yb = _dilated_attention(qkv, bias)
    mk, mv = _mem_kv(mem, row(mem_norm_w), m_w_kv.astype(BF16), row(m_k_norm_w))
    out = _main_block(x.reshape(n, D_MODEL), yb.reshape(n, B_GROUP_WIDTH), mk, mv, row(norm_w), w_bf, gate_b,
                      row(a_v_norm_w), row(a_v_norm_b), a_spatial_w, sb_full, row(m_q_norm_w),
                      proj_a.astype(BF16), proj_b.astype(BF16), proj_m.astype(BF16), w_out.astype(BF16), s)
    return out.reshape(bn, s, D_MODEL)
```

```python
import functools
import math

import jax
import jax.numpy as jnp
import numpy as np
from jax import lax
from jax.experimental import pallas as pl
from jax.experimental.pallas import tpu as pltpu

EPS = 1e-6
D_MODEL = 1024
A_WIDTH = 768
A_GROUPS = 4
A_GROUP_DIM = A_WIDTH // A_GROUPS
CHUNK = 128
B_PATTERNS = ((128, 1), (512, 4), (2048, 16))
B_DILATIONS = tuple(d for _, d in B_PATTERNS)
B_GROUPS = 3
B_HEADS_PER_GROUP = 4
B_HEAD_DIM = 64
B_GROUP_WIDTH = B_HEADS_PER_GROUP * B_HEAD_DIM
B_QKV_WIDTH = B_GROUPS * B_GROUP_WIDTH
BLOCK = 128
MEM_LEN = 256
M_HEADS = 4
M_HEAD_DIM = 128
M_WIDTH = M_HEADS * M_HEAD_DIM
REL_BUCKETS = 32
REL_MAX_DISTANCE = 2048
NEG = -1e30
LOG2E = math.log2(math.e)

OFF_BQ = 2304
OFF_BZ = 4608

LANES = 128
SLABS = B_GROUP_WIDTH // LANES
VMEM_LIMIT_BYTES = 60 * 1024 * 1024

QKV_TM = 1024
MAIN_TM = 1024
MERGE_ROWS = 256
MEMKV_BATCH = 4
MAIN_SUBTILES = 2
ATTN_IN_FLIGHT = (15, 16, 16)
assert OFF_BQ % (OFF_BZ - OFF_BQ) == 0

BF16 = jnp.bfloat16
F32 = jnp.float32


def _dot(a, b):
    return jnp.dot(a, b, preferred_element_type=F32)


def _dot_nt(a, b):
    return lax.dot_general(a, b, (((1,), (1,)), ((), ())), preferred_element_type=F32)


def _rms_rows(xf):
    return xf * lax.rsqrt(jnp.mean(xf * xf, axis=-1, keepdims=True) + EPS)


def _gelu_exact(t):
    return 0.5 * t * (1.0 + lax.erf(t * (2.0 ** -0.5)))


def _interleave(stage_generators):
    done = object()
    queue, live = list(stage_generators), []
    while queue or live:
        if queue:
            live.append(queue.pop(0))
        live = [gen for gen in live if next(gen, done) is not done]


def _const_spec(shape):
    nd = len(shape)
    return pl.BlockSpec(shape, lambda *_: (0,) * nd, pipeline_mode=pl.Buffered(1))


def _t5_causal_bucket(dist):
    max_exact = REL_BUCKETS // 2
    df = np.maximum(dist, 1).astype(np.float32)
    scaled = (np.log(df / np.float32(max_exact)) / np.float32(math.log(REL_MAX_DISTANCE / max_exact))
              * np.float32(REL_BUCKETS - max_exact))
    large = np.minimum(max_exact + scaled.astype(np.int32), REL_BUCKETS - 1)
    return np.where(dist < max_exact, dist, large).astype(np.int32)


def _bias_kernel(bucket_ref, valid_ref, rb_ref, out_ref):
    g = pl.program_id(0)
    bk = bucket_ref[0]
    valid = valid_ref[...] != 0
    for h in range(B_HEADS_PER_GROUP):
        acc = jnp.zeros((BLOCK, 2 * BLOCK), F32)
        for b in range(REL_BUCKETS):
            acc = jnp.where(bk == b, rb_ref[b, g * B_HEADS_PER_GROUP + h], acc)
        out_ref[0, h * BLOCK:(h + 1) * BLOCK, :] = jnp.where(valid, acc * LOG2E, NEG)


def _expand_bias(rel_bias):
    qi = np.arange(BLOCK, dtype=np.int32)[:, None]
    kj = np.arange(2 * BLOCK, dtype=np.int32)[None, :]
    step = qi + BLOCK - kj
    buckets = np.stack([_t5_causal_bucket(np.maximum(step, 0) * d) for d in B_DILATIONS])
    valid = ((step >= 0) & (step <= BLOCK)).astype(np.int32)
    return pl.pallas_call(
        _bias_kernel,
        grid=(B_GROUPS,),
        in_specs=[
            pl.BlockSpec((1, BLOCK, 2 * BLOCK), lambda g: (g, 0, 0)),
            pl.BlockSpec((BLOCK, 2 * BLOCK), lambda g: (0, 0)),
            pl.BlockSpec(memory_space=pltpu.SMEM),
        ],
        out_specs=pl.BlockSpec((1, B_HEADS_PER_GROUP * BLOCK, 2 * BLOCK), lambda g: (g, 0, 0)),
        out_shape=jax.ShapeDtypeStruct((B_GROUPS, B_HEADS_PER_GROUP * BLOCK, 2 * BLOCK), F32),
        name="rel_bias_expand",
    )(buckets, valid, rel_bias)


def _qkv_kernel(x_ref, nw_ref, w_ref, wq_ref, wk_ref, *refs, tm):
    out_refs = refs[:3 * B_GROUPS]
    scratch = refs[3 * B_GROUPS:]
    h = (_rms_rows(x_ref[0]) * nw_ref[...]).astype(BF16)
    r = lax.broadcasted_iota(jnp.int32, (B_GROUP_WIDTH, B_GROUP_WIDTH), 0) // B_HEAD_DIM
    c = lax.broadcasted_iota(jnp.int32, (B_GROUP_WIDTH, B_GROUP_WIDTH), 1) // B_HEAD_DIM
    seg = jnp.where(r == c, 1.0 / B_HEAD_DIM, 0.0).astype(BF16)

    def head_norm(t, w):
        ms = _dot((t * t).astype(BF16), seg)
        return t * lax.rsqrt(ms + EPS) * w

    def emit(g, which, t):
        out = out_refs[3 * g + which]
        d = B_DILATIONS[g]
        if d == 1:
            out[0] = t.astype(BF16)
            return
        scr = scratch[3 * (g - 1) + which]
        for j in range(SLABS):
            scr[j] = t[:, j * LANES:(j + 1) * LANES]
        for res in range(d):
            for j in range(SLABS):
                out[0, res, :, j * LANES:(j + 1) * LANES] = (
                    scr[j, pl.ds(res, tm // d, stride=d), :].astype(BF16))

    def piece(g, which):
        lo = which * B_QKV_WIDTH + g * B_GROUP_WIDTH
        return _dot(h, w_ref[:, lo:lo + B_GROUP_WIDTH])

    qk = [[piece(g, which) for which in range(2)] for g in range(B_GROUPS)]
    for g in reversed(range(B_GROUPS)):
        emit(g, 0, head_norm(qk[g][0], wq_ref[...]))
        emit(g, 1, head_norm(qk[g][1], wk_ref[...]))
        emit(g, 2, piece(g, 2))


def _qkv_proj(x, norm_w, w_qkv, wq_t, wk_t):
    bn, s, _ = x.shape
    tm = QKV_TM
    out_shapes, out_specs, scratch = [], [], []
    for d in B_DILATIONS:
        if d == 1:
            shape = jax.ShapeDtypeStruct((bn, s, B_GROUP_WIDTH), BF16)
            spec = pl.BlockSpec((1, tm, B_GROUP_WIDTH), lambda b, j: (b, j, 0))
        else:
            shape = jax.ShapeDtypeStruct((bn, d, s // d, B_GROUP_WIDTH), BF16)
            spec = pl.BlockSpec((1, d, tm // d, B_GROUP_WIDTH), lambda b, j: (b, 0, j, 0))
            scratch += [pltpu.VMEM((SLABS, tm, LANES), F32)] * 3
        out_shapes += [shape] * 3
        out_specs += [spec] * 3
    return pl.pallas_call(
        functools.partial(_qkv_kernel, tm=tm),
        grid=(bn, s // tm),
        in_specs=[
            pl.BlockSpec((1, tm, D_MODEL), lambda b, j: (b, j, 0)),
            _const_spec((1, D_MODEL)),
            pl.BlockSpec((D_MODEL, OFF_BZ - OFF_BQ), lambda b, j: (0, OFF_BQ // (OFF_BZ - OFF_BQ)),
                         pipeline_mode=pl.Buffered(1)),
            _const_spec((1, B_GROUP_WIDTH)),
            _const_spec((1, B_GROUP_WIDTH)),
        ],
        out_specs=out_specs,
        out_shape=out_shapes,
        scratch_shapes=scratch,
        compiler_params=pltpu.CompilerParams(
            dimension_semantics=("parallel", "parallel"), vmem_limit_bytes=VMEM_LIMIT_BYTES),
        name="qkv_proj",
    )(x, norm_w, w_qkv, wq_t, wk_t)


def _attn_kernel(q0, k0, v0, q1, k1, v1, q2, k2, v2, bias_ref, yb_ref, o_scr, lse_scr, *, s):
    lane_head = lax.broadcasted_iota(jnp.int32, (1, B_GROUP_WIDTH), 1) // B_HEAD_DIM
    masks = [lane_head == h for h in range(B_HEADS_PER_GROUP)]

    def block(g, q, k, v, n, start):
        if n is None:
            qb, kb, vb = q[0:BLOCK, :], k[0:BLOCK, :], v[0:BLOCK, :]
            bias = bias_ref[g, :, BLOCK:2 * BLOCK]
        else:
            aligned = (lambda t: t) if isinstance(n, int) else (lambda t: pl.multiple_of(t, BLOCK))
            q_lo = aligned(n * BLOCK)
            k_lo = aligned((n - 1) * BLOCK)
            qb, kb, vb = q[pl.ds(q_lo, BLOCK), :], k[pl.ds(k_lo, 2 * BLOCK), :], v[pl.ds(k_lo, 2 * BLOCK), :]
            bias = bias_ref[g]
        qs = jnp.concatenate([jnp.where(m, qb, jnp.zeros_like(qb)) for m in masks], axis=0)
        sc = _dot_nt(qs, kb) + bias
        yield
        mx = jnp.max(sc, axis=-1, keepdims=True)
        p = jnp.exp2(sc - mx)
        l = jnp.sum(p, axis=-1, keepdims=True)
        yield
        o = _dot(p.astype(BF16), vb) / l
        lse = mx + jnp.log2(l)
        o_out = jnp.zeros((BLOCK, B_GROUP_WIDTH), F32)
        lse_out = jnp.zeros((BLOCK, B_GROUP_WIDTH), F32)
        for h, m in enumerate(masks):
            rows = slice(h * BLOCK, (h + 1) * BLOCK)
            o_out = jnp.where(m, o[rows], o_out)
            lse_out = jnp.where(m, lse[rows], lse_out)
        d = B_DILATIONS[g]
        idx = pl.ds(start, BLOCK) if d == 1 else pl.ds(start, BLOCK, stride=d)
        for j in range(SLABS):
            o_scr[g, j, idx, :] = o_out[:, j * LANES:(j + 1) * LANES]
            lse_scr[g, j, idx, :] = lse_out[:, j * LANES:(j + 1) * LANES]

    _interleave([block(0, q0.at[0], k0.at[0], v0.at[0], None, 0)])

    def g0_body(it, carry):
        blocks = []
        for c in range(ATTN_IN_FLIGHT[0]):
            n = 1 + it * ATTN_IN_FLIGHT[0] + c
            blocks.append(block(0, q0.at[0], k0.at[0], v0.at[0], n, pl.multiple_of(n * BLOCK, BLOCK)))
        _interleave(blocks)
        return carry

    lax.fori_loop(0, (s // BLOCK - 1) // ATTN_IN_FLIGHT[0], g0_body, 0)

    for g, (q, k, v) in ((1, (q1, k1, v1)), (2, (q2, k2, v2))):
        d = B_DILATIONS[g]
        nb = s // d // BLOCK
        per_iter = ATTN_IN_FLIGHT[g] // nb

        def res_body(it, carry, g=g, q=q, k=k, v=v, d=d, nb=nb, per_iter=per_iter):
            blocks = []
            for c in range(per_iter):
                res = it * per_iter + c
                qr, kr, vr = q.at[0, res], k.at[0, res], v.at[0, res]
                blocks.append(block(g, qr, kr, vr, None, res))
                blocks += [block(g, qr, kr, vr, n, res + n * (BLOCK * d)) for n in range(1, nb)]
            _interleave(blocks)
            return carry

        lax.fori_loop(0, d // per_iter, res_body, 0)

    def merge_body(c, carry):
        rows = pl.ds(pl.multiple_of(c * MERGE_ROWS, MERGE_ROWS), MERGE_ROWS)
        for j in range(SLABS):
            l0, l1, l2 = lse_scr[0, j, rows, :], lse_scr[1, j, rows, :], lse_scr[2, j, rows, :]
            lm = jnp.maximum(jnp.maximum(l0, l1), l2)
            e0, e1, e2 = jnp.exp2(l0 - lm), jnp.exp2(l1 - lm), jnp.exp2(l2 - lm)
            y = (e0 * o_scr[0, j, rows, :] + e1 * o_scr[1, j, rows, :] + e2 * o_scr[2, j, rows, :]) / (e0 + e1 + e2)
            yb_ref[0, rows, j * LANES:(j + 1) * LANES] = y.astype(yb_ref.dtype)
        return carry

    lax.fori_loop(0, s // MERGE_ROWS, merge_body, 0)


def _dilated_attention(qkv, bias):
    bn, s, _ = qkv[0].shape
    in_specs = []
    for d in B_DILATIONS:
        if d == 1:
            spec = pl.BlockSpec((1, s, B_GROUP_WIDTH), lambda b: (b, 0, 0))
        else:
            spec = pl.BlockSpec((1, d, s // d, B_GROUP_WIDTH), lambda b: (b, 0, 0, 0))
        in_specs += [spec] * 3
    in_specs.append(_const_spec((B_GROUPS, B_HEADS_PER_GROUP * BLOCK, 2 * BLOCK)))
    return pl.pallas_call(
        functools.partial(_attn_kernel, s=s),
        grid=(bn,),
        in_specs=in_specs,
        out_specs=pl.BlockSpec((1, s, B_GROUP_WIDTH), lambda b: (b, 0, 0)),
        out_shape=jax.ShapeDtypeStruct((bn, s, B_GROUP_WIDTH), BF16),
        scratch_shapes=[pltpu.VMEM((B_GROUPS, SLABS, s, LANES), F32),
                        pltpu.VMEM((B_GROUPS, SLABS, s, LANES), F32)],
        compiler_params=pltpu.CompilerParams(
            dimension_semantics=("parallel",), vmem_limit_bytes=VMEM_LIMIT_BYTES),
        name="dilated_attn",
    )(*qkv, bias)


def _memkv_kernel(mem_ref, nw_ref, w_ref, kw_ref, mk_ref, mv_ref):
    rows = mem_ref.shape[0] * MEM_LEN
    hm = (_rms_rows(mem_ref[...].reshape(rows, D_MODEL)) * nw_ref[...]).astype(BF16)
    kv = _dot(hm, w_ref[...])
    for h in range(M_HEADS):
        cols = slice(h * M_HEAD_DIM, (h + 1) * M_HEAD_DIM)
        mk = (_rms_rows(kv[:, cols]) * kw_ref[...]).astype(BF16)
        mk_ref[:, :, cols] = mk.reshape(mem_ref.shape[0], MEM_LEN, M_HEAD_DIM)
    mv_ref[...] = kv[:, M_WIDTH:].astype(BF16).reshape(mv_ref.shape)


def _mem_kv(mem, mem_norm_w, w_kv, k_norm_w):
    bn = mem.shape[0]
    mb = math.gcd(bn, MEMKV_BATCH)
    out = jax.ShapeDtypeStruct((bn, MEM_LEN, M_WIDTH), BF16)
    ospec = pl.BlockSpec((mb, MEM_LEN, M_WIDTH), lambda b: (b, 0, 0))
    return pl.pallas_call(
        _memkv_kernel,
        grid=(bn // mb,),
        in_specs=[
            pl.BlockSpec((mb, MEM_LEN, D_MODEL), lambda b: (b, 0, 0)),
            _const_spec((1, D_MODEL)),
            _const_spec((D_MODEL, 2 * M_WIDTH)),
            _const_spec((1, M_HEAD_DIM)),
        ],
        out_specs=[ospec, ospec],
        out_shape=[out, out],
        compiler_params=pltpu.CompilerParams(dimension_semantics=("parallel",)),
        name="mem_kv",
    )(mem, mem_norm_w, w_kv, k_norm_w)


MW_AU, MW_AV, MW_AZ = 0, 768, 1536
MW_BZ, MW_MQ, MW_MZ = 4608, 4864, 5376
MW_G = 5888
MW_TOTAL = MW_G + 3 * D_MODEL


def _main_kernel(x_ref, yb_ref, mk_ref, mv_ref,
                 nw_ref, w_ref, gb_ref, lnw_ref, lnb_ref, sw_ref, sb_ref, mqw_ref,
                 pa_ref, pb_ref, pm_ref, wo_ref, out_ref, *, tm):
    ti = lax.broadcasted_iota(jnp.int32, (CHUNK, CHUNK), 0)
    si = lax.broadcasted_iota(jnp.int32, (CHUNK, CHUNK), 1)
    ws = [jnp.where(si <= ti, sw_ref[g], 0.0).astype(BF16) for g in range(A_GROUPS)]
    win_lo = [(g * A_GROUP_DIM) // LANES * LANES for g in range(A_GROUPS)]
    lane = lax.broadcasted_iota(jnp.int32, (1, LANES), 1)
    in_first = lane < (A_GROUP_DIM - LANES)

    def sub_tile(rows):
        n_rows = rows.stop - rows.start
        xf = x_ref[rows, :]
        h = (_rms_rows(xf) * nw_ref[...]).astype(BF16)

        def proj(lo, width):
            return _dot(h, w_ref[:, lo:lo + width])

        def gate(i):
            return jax.nn.sigmoid(proj(MW_G + i * D_MODEL, D_MODEL) + gb_ref[i:i + 1, :])

        def mem_head(mq, hd):
            cols = slice(hd * M_HEAD_DIM, (hd + 1) * M_HEAD_DIM)
            qn = (_rms_rows(mq[:, cols]) * mqw_ref[...]).astype(BF16)
            sc = _dot_nt(qn, mk_ref[0, :, cols]) * (M_HEAD_DIM ** -0.5)
            sc = sc - jnp.max(sc, axis=-1, keepdims=True)
            p = jnp.exp(sc)
            l = jnp.sum(p, axis=-1, keepdims=True)
            return _dot(p.astype(BF16), mv_ref[0, :, cols]) / l

        a_u = proj(MW_AU, A_WIDTH)
        a_v = proj(MW_AV, A_WIDTH)
        yield
        u = _gelu_exact(a_u)
        gv = _gelu_exact(a_v)
        yield
        mq = proj(MW_MQ, M_WIDTH)
        a_z = proj(MW_AZ, A_WIDTH)
        yield
        mu = jnp.mean(gv, axis=-1, keepdims=True)
        gc = gv - mu
        vv = gc * lax.rsqrt(jnp.mean(gc * gc, axis=-1, keepdims=True) + EPS) * lnw_ref[...] + lnb_ref[...]
        vv = vv.astype(BF16)
        mixed_chunks = []
        for c in range(n_rows // CHUNK):
            vc = vv[c * CHUNK:(c + 1) * CHUNK, :]
            m = [_dot(ws[g], vc[:, win_lo[g]:win_lo[g] + 2 * LANES]) for g in range(A_GROUPS)]
            lo, hi = (lambda t: t[:, :LANES]), (lambda t: t[:, LANES:])
            pieces = [lo(m[0]), jnp.where(in_first, hi(m[0]), lo(m[1])), hi(m[1]),
                      lo(m[2]), jnp.where(in_first, hi(m[2]), lo(m[3])), hi(m[3])]
            mixed_chunks.append(jnp.concatenate(pieces, axis=-1) + sb_ref[...])
        mixed = jnp.concatenate(mixed_chunks, axis=0)
        y_a = (u * mixed * jax.nn.silu(a_z)).astype(BF16)
        yield
        acc = gate(0) * _dot(y_a, pa_ref[...])
        yield

        ym_heads = [mem_head(mq, hd) for hd in range(M_HEADS)]
        yield
        m_z = proj(MW_MZ, M_WIDTH)
        b_z = proj(MW_BZ, B_GROUP_WIDTH)
        yield
        y_m = (jnp.concatenate(ym_heads, axis=-1) * jax.nn.silu(m_z)).astype(BF16)
        y_b = (yb_ref[rows, :].astype(F32) * jax.nn.silu(b_z)).astype(BF16)
        yield
        acc = acc + gate(2) * _dot(y_m, pm_ref[...])
        yield
        acc = acc + gate(1) * _dot(y_b, pb_ref[...])
        yield
        out_ref[rows, :] = xf + _dot(acc.astype(BF16), wo_ref[...])

    rows_per = tm // MAIN_SUBTILES
    _interleave([sub_tile(slice(k * rows_per, (k + 1) * rows_per)) for k in range(MAIN_SUBTILES)])


def _main_block(x2, yb, mk, mv, norm_w, w_main, gate_b, lnw, lnb, sw, sb_full, mqw,
                proj_a, proj_b, proj_m, w_out, s):
    n = x2.shape[0]
    tm = MAIN_TM
    steps_per_batch = s // tm
    tile = lambda width: pl.BlockSpec((tm, width), lambda i: (i, 0))
    mem_spec = pl.BlockSpec((1, MEM_LEN, M_WIDTH), lambda i: (i // steps_per_batch, 0, 0))
    return pl.pallas_call(
        functools.partial(_main_kernel, tm=tm),
        grid=(n // tm,),
        in_specs=[tile(D_MODEL), tile(B_GROUP_WIDTH), mem_spec, mem_spec,
                  _const_spec((1, D_MODEL)),
                  _const_spec((D_MODEL, MW_TOTAL)),
                  _const_spec((3, D_MODEL)),
                  _const_spec((1, A_WIDTH)),
                  _const_spec((1, A_WIDTH)),
                  _const_spec((A_GROUPS, CHUNK, CHUNK)),
                  _const_spec((CHUNK, A_WIDTH)),
                  _const_spec((1, M_HEAD_DIM)),
                  _const_spec((A_WIDTH, D_MODEL)),
                  _const_spec((B_GROUP_WIDTH, D_MODEL)),
                  _const_spec((M_WIDTH, D_MODEL)),
                  _const_spec((D_MODEL, D_MODEL))],
        out_specs=tile(D_MODEL),
        out_shape=jax.ShapeDtypeStruct((n, D_MODEL), F32),
        compiler_params=pltpu.CompilerParams(
            dimension_semantics=("parallel",), vmem_limit_bytes=VMEM_LIMIT_BYTES),
        name="main_block",
    )(x2, yb, mk, mv, norm_w, w_main, gate_b, lnw, lnb, sw, sb_full, mqw,
      proj_a, proj_b, proj_m, w_out)


def kernel(x, mem, norm_w, w_in, gate_b, a_v_norm_w, a_v_norm_b, a_spatial_w, a_spatial_b,
           b_q_norm_w, b_k_norm_w, rel_bias, mem_norm_w, m_w_kv, m_q_norm_w, m_k_norm_w,
           proj_a, proj_b, proj_m, w_out):
    bn, s, _ = x.shape
    n = bn * s
    row = lambda v: v.reshape(1, -1)

    w_bf = w_in.astype(BF16)
    wq_t = row(jnp.tile(b_q_norm_w * (B_HEAD_DIM ** -0.5 * LOG2E), B_HEADS_PER_GROUP))
    wk_t = row(jnp.tile(b_k_norm_w, B_HEADS_PER_GROUP))
    sb_full = jnp.repeat(a_spatial_b.T, A_GROUP_DIM, axis=1)

    bias = _expand_bias(rel_bias)
    qkv = _qkv_proj(x, row(norm_w), w_bf, wq_t, wk_t)
    yb = _dilated_attention(qkv, bias)
    mk, mv = _mem_kv(mem, row(mem_norm_w), m_w_kv.astype(BF16), row(m_k_norm_w))
    out = _main_block(x.reshape(n, D_MODEL), yb.reshape(n, B_GROUP_WIDTH), mk, mv, row(norm_w), w_bf, gate_b,
                      row(a_v_norm_w), row(a_v_norm_b), a_spatial_w, sb_full, row(m_q_norm_w),
                      proj_a.astype(BF16), proj_b.astype(BF16), proj_m.astype(BF16), w_out.astype(BF16), s)
    return out.reshape(bn, s, D_MODEL)
```

```python
import functools
import math

import jax
import jax.numpy as jnp
import numpy as np
from jax import lax
from jax.experimental import pallas as pl
from jax.experimental.pallas import tpu as pltpu

EPS = 1e-6
D_MODEL = 1024
A_WIDTH = 768
A_GROUPS = 4
A_GROUP_DIM = A_WIDTH // A_GROUPS
CHUNK = 128
B_PATTERNS = ((128, 1), (512, 4), (2048, 16))
B_DILATIONS = tuple(d for _, d in B_PATTERNS)
B_GROUPS = 3
B_HEADS_PER_GROUP = 4
B_HEAD_DIM = 64
B_GROUP_WIDTH = B_HEADS_PER_GROUP * B_HEAD_DIM
B_QKV_WIDTH = B_GROUPS * B_GROUP_WIDTH
BLOCK = 128
MEM_LEN = 256
M_HEADS = 4
M_HEAD_DIM = 128
M_WIDTH = M_HEADS * M_HEAD_DIM
REL_BUCKETS = 32
REL_MAX_DISTANCE = 2048
NEG = -1e30
LOG2E = math.log2(math.e)

OFF_BQ = 2304
OFF_BZ = 4608

LANES = 128
SLABS = B_GROUP_WIDTH // LANES
VMEM_LIMIT_BYTES = 60 * 1024 * 1024

QKV_TM = 1024
MAIN_TM = 1024
MERGE_ROWS = 256
MEMKV_BATCH = 4
MAIN_SUBTILES = 2
QKV_SUBTILES = 2
ATTN_IN_FLIGHT = (15, 16, 16)
assert OFF_BQ % (OFF_BZ - OFF_BQ) == 0

BF16 = jnp.bfloat16
F32 = jnp.float32


def _dot(a, b):
    return jnp.dot(a, b, preferred_element_type=F32)


def _dot_nt(a, b):
    return lax.dot_general(a, b, (((1,), (1,)), ((), ())), preferred_element_type=F32)


def _rms_rows(xf):
    return xf * lax.rsqrt(jnp.mean(xf * xf, axis=-1, keepdims=True) + EPS)


def _gelu_exact(t):
    return 0.5 * t * (1.0 + lax.erf(t * (2.0 ** -0.5)))


def _interleave(stage_generators):
    done = object()
    queue, live = list(stage_generators), []
    while queue or live:
        if queue:
            live.append(queue.pop(0))
        live = [gen for gen in live if next(gen, done) is not done]


def _const_spec(shape):
    nd = len(shape)
    return pl.BlockSpec(shape, lambda *_: (0,) * nd, pipeline_mode=pl.Buffered(1))


def _t5_causal_bucket(dist):
    max_exact = REL_BUCKETS // 2
    df = np.maximum(dist, 1).astype(np.float32)
    scaled = (np.log(df / np.float32(max_exact)) / np.float32(math.log(REL_MAX_DISTANCE / max_exact))
              * np.float32(REL_BUCKETS - max_exact))
    large = np.minimum(max_exact + scaled.astype(np.int32), REL_BUCKETS - 1)
    return np.where(dist < max_exact, dist, large).astype(np.int32)


def _bias_kernel(bucket_ref, valid_ref, rb_ref, out_ref):
    g = pl.program_id(0)
    bk = bucket_ref[0]
    valid = valid_ref[...] != 0
    for h in range(B_HEADS_PER_GROUP):
        acc = jnp.zeros((BLOCK, 2 * BLOCK), F32)
        for b in range(REL_BUCKETS):
            acc = jnp.where(bk == b, rb_ref[b, g * B_HEADS_PER_GROUP + h], acc)
        out_ref[0, h * BLOCK:(h + 1) * BLOCK, :] = jnp.where(valid, acc * LOG2E, NEG)


def _expand_bias(rel_bias):
    qi = np.arange(BLOCK, dtype=np.int32)[:, None]
    kj = np.arange(2 * BLOCK, dtype=np.int32)[None, :]
    step = qi + BLOCK - kj
    buckets = np.stack([_t5_causal_bucket(np.maximum(step, 0) * d) for d in B_DILATIONS])
    valid = ((step >= 0) & (step <= BLOCK)).astype(np.int32)
    return pl.pallas_call(
        _bias_kernel,
        grid=(B_GROUPS,),
        in_specs=[
            pl.BlockSpec((1, BLOCK, 2 * BLOCK), lambda g: (g, 0, 0)),
            pl.BlockSpec((BLOCK, 2 * BLOCK), lambda g: (0, 0)),
            pl.BlockSpec(memory_space=pltpu.SMEM),
        ],
        out_specs=pl.BlockSpec((1, B_HEADS_PER_GROUP * BLOCK, 2 * BLOCK), lambda g: (g, 0, 0)),
        out_shape=jax.ShapeDtypeStruct((B_GROUPS, B_HEADS_PER_GROUP * BLOCK, 2 * BLOCK), F32),
        name="rel_bias_expand",
    )(buckets, valid, rel_bias)


def _qkv_kernel(x_ref, nw_ref, w_ref, wq_ref, wk_ref, *refs, tm):
    out_refs = refs[:3 * B_GROUPS]
    scratch = refs[3 * B_GROUPS:]
    n_slabs = 3 * (B_GROUPS - 1)
    rows = tm // QKV_SUBTILES
    r = lax.broadcasted_iota(jnp.int32, (B_GROUP_WIDTH, B_GROUP_WIDTH), 0) // B_HEAD_DIM
    c = lax.broadcasted_iota(jnp.int32, (B_GROUP_WIDTH, B_GROUP_WIDTH), 1) // B_HEAD_DIM
    seg = jnp.where(r == c, 1.0 / B_HEAD_DIM, 0.0).astype(BF16)

    def head_norm(t, w):
        ms = _dot((t * t).astype(BF16), seg)
        return t * lax.rsqrt(ms + EPS) * w

    def sub_tile(k):
        h = (_rms_rows(x_ref[0, k * rows:(k + 1) * rows, :]) * nw_ref[...]).astype(BF16)

        def emit(g, which, t):
            out = out_refs[3 * g + which]
            d = B_DILATIONS[g]
            sub = rows // d
            if d == 1:
                out[0, k * rows:(k + 1) * rows, :] = t.astype(BF16)
                return
            scr = scratch[k * n_slabs + 3 * (g - 1) + which]
            for j in range(SLABS):
                scr[j] = t[:, j * LANES:(j + 1) * LANES]
            for res in range(d):
                for j in range(SLABS):
                    out[0, res, k * sub:(k + 1) * sub, j * LANES:(j + 1) * LANES] = (
                        scr[j, pl.ds(res, sub, stride=d), :].astype(BF16))

        def piece(g, which):
            lo = which * B_QKV_WIDTH + g * B_GROUP_WIDTH
            return _dot(h, w_ref[:, lo:lo + B_GROUP_WIDTH])

        qk = [[piece(g, which) for which in range(2)] for g in range(B_GROUPS)]
        yield
        for g in reversed(range(B_GROUPS)):
            emit(g, 0, head_norm(qk[g][0], wq_ref[...]))
            emit(g, 1, head_norm(qk[g][1], wk_ref[...]))
            emit(g, 2, piece(g, 2))
            yield

    _interleave([sub_tile(k) for k in range(QKV_SUBTILES)])


def _qkv_proj(x, norm_w, w_qkv, wq_t, wk_t):
    bn, s, _ = x.shape
    tm = QKV_TM
    out_shapes, out_specs, scratch = [], [], []
    for d in B_DILATIONS:
        if d == 1:
            shape = jax.ShapeDtypeStruct((bn, s, B_GROUP_WIDTH), BF16)
            spec = pl.BlockSpec((1, tm, B_GROUP_WIDTH), lambda b, j: (b, j, 0))
        else:
            shape = jax.ShapeDtypeStruct((bn, d, s // d, B_GROUP_WIDTH), BF16)
            spec = pl.BlockSpec((1, d, tm // d, B_GROUP_WIDTH), lambda b, j: (b, 0, j, 0))
            scratch += [pltpu.VMEM((SLABS, tm // QKV_SUBTILES, LANES), F32)] * 3
        out_shapes += [shape] * 3
        out_specs += [spec] * 3
    return pl.pallas_call(
        functools.partial(_qkv_kernel, tm=tm),
        grid=(bn, s // tm),
        in_specs=[
            pl.BlockSpec((1, tm, D_MODEL), lambda b, j: (b, j, 0)),
            _const_spec((1, D_MODEL)),
            pl.BlockSpec((D_MODEL, OFF_BZ - OFF_BQ), lambda b, j: (0, OFF_BQ // (OFF_BZ - OFF_BQ)),
                         pipeline_mode=pl.Buffered(1)),
            _const_spec((1, B_GROUP_WIDTH)),
            _const_spec((1, B_GROUP_WIDTH)),
        ],
        out_specs=out_specs,
        out_shape=out_shapes,
        scratch_shapes=scratch * QKV_SUBTILES,
        compiler_params=pltpu.CompilerParams(
            dimension_semantics=("parallel", "parallel"), vmem_limit_bytes=VMEM_LIMIT_BYTES),
        name="qkv_proj",
    )(x, norm_w, w_qkv, wq_t, wk_t)


def _attn_kernel(q0, k0, v0, q1, k1, v1, q2, k2, v2, bias_ref, yb_ref, o_scr, lse_scr, *, s):
    lane_head = lax.broadcasted_iota(jnp.int32, (1, B_GROUP_WIDTH), 1) // B_HEAD_DIM
    masks = [lane_head == h for h in range(B_HEADS_PER_GROUP)]

    def block(g, q, k, v, n, start):
        if n is None:
            qb, kb, vb = q[0:BLOCK, :], k[0:BLOCK, :], v[0:BLOCK, :]
            bias = bias_ref[g, :, BLOCK:2 * BLOCK]
        else:
            aligned = (lambda t: t) if isinstance(n, int) else (lambda t: pl.multiple_of(t, BLOCK))
            q_lo = aligned(n * BLOCK)
            k_lo = aligned((n - 1) * BLOCK)
            qb, kb, vb = q[pl.ds(q_lo, BLOCK), :], k[pl.ds(k_lo, 2 * BLOCK), :], v[pl.ds(k_lo, 2 * BLOCK), :]
            bias = bias_ref[g]
        qs = jnp.concatenate([jnp.where(m, qb, jnp.zeros_like(qb)) for m in masks], axis=0)
        sc = _dot_nt(qs, kb) + bias
        yield
        mx = jnp.max(sc, axis=-1, keepdims=True)
        p = jnp.exp2(sc - mx)
        l = jnp.sum(p, axis=-1, keepdims=True)
        yield
        o = _dot(p.astype(BF16), vb) / l
        lse = mx + jnp.log2(l)
        o_out = jnp.zeros((BLOCK, B_GROUP_WIDTH), F32)
        lse_out = jnp.zeros((BLOCK, B_GROUP_WIDTH), F32)
        for h, m in enumerate(masks):
            rows = slice(h * BLOCK, (h + 1) * BLOCK)
            o_out = jnp.where(m, o[rows], o_out)
            lse_out = jnp.where(m, lse[rows], lse_out)
        d = B_DILATIONS[g]
        idx = pl.ds(start, BLOCK) if d == 1 else pl.ds(start, BLOCK, stride=d)
        for j in range(SLABS):
            o_scr[g, j, idx, :] = o_out[:, j * LANES:(j + 1) * LANES]
            lse_scr[g, j, idx, :] = lse_out[:, j * LANES:(j + 1) * LANES]

    _interleave([block(0, q0.at[0], k0.at[0], v0.at[0], None, 0)])

    def g0_body(it, carry):
        blocks = []
        for c in range(ATTN_IN_FLIGHT[0]):
            n = 1 + it * ATTN_IN_FLIGHT[0] + c
            blocks.append(block(0, q0.at[0], k0.at[0], v0.at[0], n, pl.multiple_of(n * BLOCK, BLOCK)))
        _interleave(blocks)
        return carry

    lax.fori_loop(0, (s // BLOCK - 1) // ATTN_IN_FLIGHT[0], g0_body, 0)

    for g, (q, k, v) in ((1, (q1, k1, v1)), (2, (q2, k2, v2))):
        d = B_DILATIONS[g]
        nb = s // d // BLOCK
        per_iter = ATTN_IN_FLIGHT[g] // nb

        def res_body(it, carry, g=g, q=q, k=k, v=v, d=d, nb=nb, per_iter=per_iter):
            blocks = []
            for c in range(per_iter):
                res = it * per_iter + c
                qr, kr, vr = q.at[0, res], k.at[0, res], v.at[0, res]
                blocks.append(block(g, qr, kr, vr, None, res))
                blocks += [block(g, qr, kr, vr, n, res + n * (BLOCK * d)) for n in range(1, nb)]
            _interleave(blocks)
            return carry

        lax.fori_loop(0, d // per_iter, res_body, 0)

    def merge_body(c, carry):
        rows = pl.ds(pl.multiple_of(c * MERGE_ROWS, MERGE_ROWS), MERGE_ROWS)
        for j in range(SLABS):
            l0, l1, l2 = lse_scr[0, j, rows, :], lse_scr[1, j, rows, :], lse_scr[2, j, rows, :]
            lm = jnp.maximum(jnp.maximum(l0, l1), l2)
            e0, e1, e2 = jnp.exp2(l0 - lm), jnp.exp2(l1 - lm), jnp.exp2(l2 - lm)
            y = (e0 * o_scr[0, j, rows, :] + e1 * o_scr[1, j, rows, :] + e2 * o_scr[2, j, rows, :]) / (e0 + e1 + e2)
            yb_ref[0, rows, j * LANES:(j + 1) * LANES] = y.astype(yb_ref.dtype)
        return carry

    lax.fori_loop(0, s // MERGE_ROWS, merge_body, 0)


def _dilated_attention(qkv, bias):
    bn, s, _ = qkv[0].shape
    in_specs = []
    for d in B_DILATIONS:
        if d == 1:
            spec = pl.BlockSpec((1, s, B_GROUP_WIDTH), lambda b: (b, 0, 0))
        else:
            spec = pl.BlockSpec((1, d, s // d, B_GROUP_WIDTH), lambda b: (b, 0, 0, 0))
        in_specs += [spec] * 3
    in_specs.append(_const_spec((B_GROUPS, B_HEADS_PER_GROUP * BLOCK, 2 * BLOCK)))
    return pl.pallas_call(
        functools.partial(_attn_kernel, s=s),
        grid=(bn,),
        in_specs=in_specs,
        out_specs=pl.BlockSpec((1, s, B_GROUP_WIDTH), lambda b: (b, 0, 0)),
        out_shape=jax.ShapeDtypeStruct((bn, s, B_GROUP_WIDTH), BF16),
        scratch_shapes=[pltpu.VMEM((B_GROUPS, SLABS, s, LANES), F32),
                        pltpu.VMEM((B_GROUPS, SLABS, s, LANES), F32)],
        compiler_params=pltpu.CompilerParams(
            dimension_semantics=("parallel",), vmem_limit_bytes=VMEM_LIMIT_BYTES),
        name="dilated_attn",
    )(*qkv, bias)


def _memkv_kernel(mem_ref, nw_ref, w_ref, kw_ref, mk_ref, mv_ref):
    rows = mem_ref.shape[0] * MEM_LEN
    hm = (_rms_rows(mem_ref[...].reshape(rows, D_MODEL)) * nw_ref[...]).astype(BF16)
    kv = _dot(hm, w_ref[...])
    for h in range(M_HEADS):
        cols = slice(h * M_HEAD_DIM, (h + 1) * M_HEAD_DIM)
        mk = (_rms_rows(kv[:, cols]) * kw_ref[...]).astype(BF16)
        mk_ref[:, :, cols] = mk.reshape(mem_ref.shape[0], MEM_LEN, M_HEAD_DIM)
    mv_ref[...] = kv[:, M_WIDTH:].astype(BF16).reshape(mv_ref.shape)


def _mem_kv(mem, mem_norm_w, w_kv, k_norm_w):
    bn = mem.shape[0]
    mb = math.gcd(bn, MEMKV_BATCH)
    out = jax.ShapeDtypeStruct((bn, MEM_LEN, M_WIDTH), BF16)
    ospec = pl.BlockSpec((mb, MEM_LEN, M_WIDTH), lambda b: (b, 0, 0))
    return pl.pallas_call(
        _memkv_kernel,
        grid=(bn // mb,),
        in_specs=[
            pl.BlockSpec((mb, MEM_LEN, D_MODEL), lambda b: (b, 0, 0)),
            _const_spec((1, D_MODEL)),
            _const_spec((D_MODEL, 2 * M_WIDTH)),
            _const_spec((1, M_HEAD_DIM)),
        ],
        out_specs=[ospec, ospec],
        out_shape=[out, out],
        compiler_params=pltpu.CompilerParams(dimension_semantics=("parallel",)),
        name="mem_kv",
    )(mem, mem_norm_w, w_kv, k_norm_w)


MW_AU, MW_AV, MW_AZ = 0, 768, 1536
MW_BZ, MW_MQ, MW_MZ = 4608, 4864, 5376
MW_G = 5888
MW_TOTAL = MW_G + 3 * D_MODEL
MW_HI = MW_TOTAL // 2
assert MW_TOTAL % 2 == 0 and MW_HI % LANES == 0 and OFF_BQ <= MW_HI <= OFF_BZ


def _main_kernel(x_ref, yb_ref, mk_ref, mv_ref,
                 nw_ref, w_ref, wh_ref, gb_ref, lnw_ref, lnb_ref, sw_ref, sb_ref, mqw_ref,
                 pa_ref, pb_ref, pm_ref, wo_ref, out_ref, *, tm):
    ti = lax.broadcasted_iota(jnp.int32, (CHUNK, CHUNK), 0)
    si = lax.broadcasted_iota(jnp.int32, (CHUNK, CHUNK), 1)
    ws = [jnp.where(si <= ti, sw_ref[g], 0.0).astype(BF16) for g in range(A_GROUPS)]
    win_lo = [(g * A_GROUP_DIM) // LANES * LANES for g in range(A_GROUPS)]
    lane = lax.broadcasted_iota(jnp.int32, (1, LANES), 1)
    in_first = lane < (A_GROUP_DIM - LANES)

    def sub_tile(rows):
        n_rows = rows.stop - rows.start
        xf = x_ref[rows, :]
        h = (_rms_rows(xf) * nw_ref[...]).astype(BF16)

        def proj(lo, width):
            ref, base = (w_ref, 0) if lo < OFF_BQ else (wh_ref, MW_HI)
            return _dot(h, ref[:, lo - base:lo - base + width])

        def gate(i):
            return jax.nn.sigmoid(proj(MW_G + i * D_MODEL, D_MODEL) + gb_ref[i:i + 1, :])

        def mem_head(mq, hd):
            cols = slice(hd * M_HEAD_DIM, (hd + 1) * M_HEAD_DIM)
            qn = (_rms_rows(mq[:, cols]) * mqw_ref[...]).astype(BF16)
            sc = _dot_nt(qn, mk_ref[0, :, cols]) * (M_HEAD_DIM ** -0.5)
            sc = sc - jnp.max(sc, axis=-1, keepdims=True)
            p = jnp.exp(sc)
            l = jnp.sum(p, axis=-1, keepdims=True)
            return _dot(p.astype(BF16), mv_ref[0, :, cols]) / l

        a_u = proj(MW_AU, A_WIDTH)
        a_v = proj(MW_AV, A_WIDTH)
        yield
        u = _gelu_exact(a_u)
        gv = _gelu_exact(a_v)
        yield
        mq = proj(MW_MQ, M_WIDTH)
        a_z = proj(MW_AZ, A_WIDTH)
        yield
        mu = jnp.mean(gv, axis=-1, keepdims=True)
        gc = gv - mu
        vv = gc * lax.rsqrt(jnp.mean(gc * gc, axis=-1, keepdims=True) + EPS) * lnw_ref[...] + lnb_ref[...]
        vv = vv.astype(BF16)
        mixed_chunks = []
        for c in range(n_rows // CHUNK):
            vc = vv[c * CHUNK:(c + 1) * CHUNK, :]
            m = [_dot(ws[g], vc[:, win_lo[g]:win_lo[g] + 2 * LANES]) for g in range(A_GROUPS)]
            lo, hi = (lambda t: t[:, :LANES]), (lambda t: t[:, LANES:])
            pieces = [lo(m[0]), jnp.where(in_first, hi(m[0]), lo(m[1])), hi(m[1]),
                      lo(m[2]), jnp.where(in_first, hi(m[2]), lo(m[3])), hi(m[3])]
            mixed_chunks.append(jnp.concatenate(pieces, axis=-1) + sb_ref[...])
        mixed = jnp.concatenate(mixed_chunks, axis=0)
        y_a = (u * mixed * jax.nn.silu(a_z)).astype(BF16)
        yield
        acc = gate(0) * _dot(y_a, pa_ref[...])
        yield

        ym_heads = [mem_head(mq, hd) for hd in range(M_HEADS)]
        yield
        m_z = proj(MW_MZ, M_WIDTH)
        b_z = proj(MW_BZ, B_GROUP_WIDTH)
        yield
        y_m = (jnp.concatenate(ym_heads, axis=-1) * jax.nn.silu(m_z)).astype(BF16)
        y_b = (yb_ref[rows, :].astype(F32) * jax.nn.silu(b_z)).astype(BF16)
        yield
        acc = acc + gate(2) * _dot(y_m, pm_ref[...])
        yield
        acc = acc + gate(1) * _dot(y_b, pb_ref[...])
        yield
        out_ref[rows, :] = xf + _dot(acc.astype(BF16), wo_ref[...])

    rows_per = tm // MAIN_SUBTILES
    _interleave([sub_tile(slice(k * rows_per, (k + 1) * rows_per)) for k in range(MAIN_SUBTILES)])


def _main_block(x2, yb, mk, mv, norm_w, w_main, gate_b, lnw, lnb, sw, sb_full, mqw,
                proj_a, proj_b, proj_m, w_out, s):
    n = x2.shape[0]
    tm = MAIN_TM
    steps_per_batch = s // tm
    tile = lambda width: pl.BlockSpec((tm, width), lambda i: (i, 0))
    mem_spec = pl.BlockSpec((1, MEM_LEN, M_WIDTH), lambda i: (i // steps_per_batch, 0, 0))
    return pl.pallas_call(
        functools.partial(_main_kernel, tm=tm),
        grid=(n // tm,),
        in_specs=[tile(D_MODEL), tile(B_GROUP_WIDTH), mem_spec, mem_spec,
                  _const_spec((1, D_MODEL)),
                  pl.BlockSpec((D_MODEL, OFF_BQ), lambda i: (0, 0), pipeline_mode=pl.Buffered(1)),
                  pl.BlockSpec((D_MODEL, MW_HI), lambda i: (0, 1), pipeline_mode=pl.Buffered(1)),
                  _const_spec((3, D_MODEL)),
                  _const_spec((1, A_WIDTH)),
                  _const_spec((1, A_WIDTH)),
                  _const_spec((A_GROUPS, CHUNK, CHUNK)),
                  _const_spec((CHUNK, A_WIDTH)),
                  _const_spec((1, M_HEAD_DIM)),
                  _const_spec((A_WIDTH, D_MODEL)),
                  _const_spec((B_GROUP_WIDTH, D_MODEL)),
                  _const_spec((M_WIDTH, D_MODEL)),
                  _const_spec((D_MODEL, D_MODEL))],
        out_specs=tile(D_MODEL),
        out_shape=jax.ShapeDtypeStruct((n, D_MODEL), F32),
        compiler_params=pltpu.CompilerParams(
            dimension_semantics=("parallel",), vmem_limit_bytes=VMEM_LIMIT_BYTES),
        name="main_block",
    )(x2, yb, mk, mv, norm_w, w_main, w_main, gate_b, lnw, lnb, sw, sb_full, mqw,
      proj_a, proj_b, proj_m, w_out)


def kernel(x, mem, norm_w, w_in, gate_b, a_v_norm_w, a_v_norm_b, a_spatial_w, a_spatial_b,
           b_q_norm_w, b_k_norm_w, rel_bias, mem_norm_w, m_w_kv, m_q_norm_w, m_k_norm_w,
           proj_a, proj_b, proj_m, w_out):
    bn, s, _ = x.shape
    n = bn * s
    row = lambda v: v.reshape(1, -1)

    w_bf = w_in.astype(BF16)
    wq_t = row(jnp.tile(b_q_norm_w * (B_HEAD_DIM ** -0.5 * LOG2E), B_HEADS_PER_GROUP))
    wk_t = row(jnp.tile(b_k_norm_w, B_HEADS_PER_GROUP))
    sb_full = jnp.repeat(a_spatial_b.T, A_GROUP_DIM, axis=1)

    bias = _expand_bias(rel_bias)
    qkv = _qkv_proj(x, row(norm_w), w_bf, wq_t, wk_t)
    yb = _dilated_attention(qkv, bias)
    mk, mv = _mem_kv(mem, row(mem_norm_w), m_w_kv.astype(BF16), row(m_k_norm_w))
    out = _main_block(x.reshape(n, D_MODEL), yb.reshape(n, B_GROUP_WIDTH), mk, mv, row(norm_w), w_bf, gate_b,
                      row(a_v_norm_w), row(a_v_norm_b), a_spatial_w, sb_full, row(m_q_norm_w),
                      proj_a.astype(BF16), proj_b.astype(BF16), proj_m.astype(BF16), w_out.astype(BF16), s)
    return out.reshape(bn, s, D_MODEL)
```

```python
import functools
import math

import jax
import jax.numpy as jnp
import numpy as np
from jax import lax
from jax.experimental import pallas as pl
from jax.experimental.pallas import tpu as pltpu

EPS = 1e-6
D_MODEL = 1024
A_WIDTH = 768
A_GROUPS = 4
A_GROUP_DIM = A_WIDTH // A_GROUPS
CHUNK = 128
B_PATTERNS = ((128, 1), (512, 4), (2048, 16))
B_DILATIONS = tuple(d for _, d in B_PATTERNS)
B_GROUPS = 3
B_HEADS_PER_GROUP = 4
B_HEAD_DIM = 64
B_GROUP_WIDTH = B_HEADS_PER_GROUP * B_HEAD_DIM
B_QKV_WIDTH = B_GROUPS * B_GROUP_WIDTH
BLOCK = 128
MEM_LEN = 256
M_HEADS = 4
M_HEAD_DIM = 128
M_WIDTH = M_HEADS * M_HEAD_DIM
REL_BUCKETS = 32
REL_MAX_DISTANCE = 2048
NEG = -1e30
LOG2E = math.log2(math.e)

OFF_BQ = 3 * A_WIDTH
OFF_BZ = OFF_BQ + 3 * B_QKV_WIDTH

LANES = 128
SLABS = B_GROUP_WIDTH // LANES
VMEM_LIMIT_BYTES = 60 * 1024 * 1024

QKV_TM = 1024
MAIN_TM = 1024
MERGE_ROWS = 256
MEMKV_BATCH = 4
MAIN_SUBTILES = 2
QKV_SUBTILES = 2
assert OFF_BQ % (OFF_BZ - OFF_BQ) == 0

BF16 = jnp.bfloat16
F32 = jnp.float32


def _dot(a, b):
    return jnp.dot(a, b, preferred_element_type=F32)


def _dot_nt(a, b):
    return lax.dot_general(a, b, (((1,), (1,)), ((), ())), preferred_element_type=F32)


def _rms_rows(xf):
    return xf * lax.rsqrt(jnp.mean(xf * xf, axis=-1, keepdims=True) + EPS)


def _gelu_exact(t):
    return 0.5 * t * (1.0 + lax.erf(t * (2.0 ** -0.5)))


def _interleave(stage_generators):
    done = object()
    queue, live = list(stage_generators), []
    while queue or live:
        if queue:
            live.append(queue.pop(0))
        live = [gen for gen in live if next(gen, done) is not done]


def _const_spec(shape):
    nd = len(shape)
    return pl.BlockSpec(shape, lambda *_: (0,) * nd, pipeline_mode=pl.Buffered(1))


def _t5_causal_bucket(dist):
    max_exact = REL_BUCKETS // 2
    df = np.maximum(dist, 1).astype(np.float32)
    scaled = (np.log(df / np.float32(max_exact)) / np.float32(math.log(REL_MAX_DISTANCE / max_exact))
              * np.float32(REL_BUCKETS - max_exact))
    large = np.minimum(max_exact + scaled.astype(np.int32), REL_BUCKETS - 1)
    return np.where(dist < max_exact, dist, large).astype(np.int32)


def _bias_kernel(bucket_ref, valid_ref, rb_ref, out_ref):
    g = pl.program_id(0)
    bk = bucket_ref[0]
    valid = valid_ref[...] != 0
    for h in range(B_HEADS_PER_GROUP):
        acc = jnp.zeros((BLOCK, 2 * BLOCK), F32)
        for b in range(REL_BUCKETS):
            acc = jnp.where(bk == b, rb_ref[b, g * B_HEADS_PER_GROUP + h], acc)
        out_ref[0, h * BLOCK:(h + 1) * BLOCK, :] = jnp.where(valid, acc * LOG2E, NEG)


def _expand_bias(rel_bias):
    qi = np.arange(BLOCK, dtype=np.int32)[:, None]
    kj = np.arange(2 * BLOCK, dtype=np.int32)[None, :]
    step = qi + BLOCK - kj
    buckets = np.stack([_t5_causal_bucket(np.maximum(step, 0) * d) for d in B_DILATIONS])
    valid = ((step >= 0) & (step <= BLOCK)).astype(np.int32)
    return pl.pallas_call(
        _bias_kernel,
        grid=(B_GROUPS,),
        in_specs=[
            pl.BlockSpec((1, BLOCK, 2 * BLOCK), lambda g: (g, 0, 0)),
            pl.BlockSpec((BLOCK, 2 * BLOCK), lambda g: (0, 0)),
            pl.BlockSpec(memory_space=pltpu.SMEM),
        ],
        out_specs=pl.BlockSpec((1, B_HEADS_PER_GROUP * BLOCK, 2 * BLOCK), lambda g: (g, 0, 0)),
        out_shape=jax.ShapeDtypeStruct((B_GROUPS, B_HEADS_PER_GROUP * BLOCK, 2 * BLOCK), F32),
        name="rel_bias_expand",
    )(buckets, valid, rel_bias)


def _qkv_kernel(x_ref, nw_ref, w_ref, wq_ref, wk_ref, *refs, tm):
    out_refs = refs[:3 * B_GROUPS]
    scratch = refs[3 * B_GROUPS:]
    n_slabs = 3 * (B_GROUPS - 1)
    rows = tm // QKV_SUBTILES
    r = lax.broadcasted_iota(jnp.int32, (B_GROUP_WIDTH, B_GROUP_WIDTH), 0) // B_HEAD_DIM
    c = lax.broadcasted_iota(jnp.int32, (B_GROUP_WIDTH, B_GROUP_WIDTH), 1) // B_HEAD_DIM
    seg = jnp.where(r == c, 1.0 / B_HEAD_DIM, 0.0).astype(BF16)

    def head_norm(t, w):
        ms = _dot((t * t).astype(BF16), seg)
        return t * lax.rsqrt(ms + EPS) * w

    def sub_tile(k):
        h = (_rms_rows(x_ref[0, k * rows:(k + 1) * rows, :]) * nw_ref[...]).astype(BF16)

        def emit(g, which, t):
            out = out_refs[3 * g + which]
            d = B_DILATIONS[g]
            sub = rows // d
            if d == 1:
                out[0, k * rows:(k + 1) * rows, :] = t.astype(BF16)
                return
            scr = scratch[k * n_slabs + 3 * (g - 1) + which]
            for j in range(SLABS):
                scr[j] = t[:, j * LANES:(j + 1) * LANES]
            for res in range(d):
                for j in range(SLABS):
                    out[0, res, k * sub:(k + 1) * sub, j * LANES:(j + 1) * LANES] = (
                        scr[j, pl.ds(res, sub, stride=d), :].astype(BF16))

        def piece(g, which):
            lo = which * B_QKV_WIDTH + g * B_GROUP_WIDTH
            return _dot(h, w_ref[:, lo:lo + B_GROUP_WIDTH])

        qk = [[piece(g, which) for which in range(2)] for g in range(B_GROUPS)]
        yield
        for g in reversed(range(B_GROUPS)):
            emit(g, 0, head_norm(qk[g][0], wq_ref[...]))
            emit(g, 1, head_norm(qk[g][1], wk_ref[...]))
            emit(g, 2, piece(g, 2))
            yield

    _interleave([sub_tile(k) for k in range(QKV_SUBTILES)])


def _qkv_proj(x, norm_w, w_qkv, wq_t, wk_t):
    bn, s, _ = x.shape
    tm = QKV_TM
    out_shapes, out_specs, scratch = [], [], []
    for d in B_DILATIONS:
        if d == 1:
            shape = jax.ShapeDtypeStruct((bn, s, B_GROUP_WIDTH), BF16)
            spec = pl.BlockSpec((1, tm, B_GROUP_WIDTH), lambda b, j: (b, j, 0))
        else:
            shape = jax.ShapeDtypeStruct((bn, d, s // d, B_GROUP_WIDTH), BF16)
            spec = pl.BlockSpec((1, d, tm // d, B_GROUP_WIDTH), lambda b, j: (b, 0, j, 0))
            scratch += [pltpu.VMEM((SLABS, tm // QKV_SUBTILES, LANES), F32)] * 3
        out_shapes += [shape] * 3
        out_specs += [spec] * 3
    return pl.pallas_call(
        functools.partial(_qkv_kernel, tm=tm),
        grid=(bn, s // tm),
        in_specs=[
            pl.BlockSpec((1, tm, D_MODEL), lambda b, j: (b, j, 0)),
            _const_spec((1, D_MODEL)),
            pl.BlockSpec((D_MODEL, OFF_BZ - OFF_BQ), lambda b, j: (0, OFF_BQ // (OFF_BZ - OFF_BQ)),
                         pipeline_mode=pl.Buffered(1)),
            _const_spec((1, B_GROUP_WIDTH)),
            _const_spec((1, B_GROUP_WIDTH)),
        ],
        out_specs=out_specs,
        out_shape=out_shapes,
        scratch_shapes=scratch * QKV_SUBTILES,
        compiler_params=pltpu.CompilerParams(
            dimension_semantics=("parallel", "parallel"), vmem_limit_bytes=VMEM_LIMIT_BYTES),
        name="qkv_proj",
    )(x, norm_w, w_qkv, wq_t, wk_t)


def _attn_kernel(q0, k0, v0, q1, k1, v1, q2, k2, v2, bias_ref, yb_ref, o_scr, lse_scr, *, s):
    lane_head = lax.broadcasted_iota(jnp.int32, (1, B_GROUP_WIDTH), 1) // B_HEAD_DIM
    masks = [lane_head == h for h in range(B_HEADS_PER_GROUP)]

    def block(g, q, k, v, n, start):
        qb = q[n * BLOCK:(n + 1) * BLOCK, :]
        if n == 0:
            kb, vb = k[0:BLOCK, :], v[0:BLOCK, :]
            bias = bias_ref[g, :, BLOCK:2 * BLOCK]
        else:
            kb, vb = k[(n - 1) * BLOCK:(n + 1) * BLOCK, :], v[(n - 1) * BLOCK:(n + 1) * BLOCK, :]
            bias = bias_ref[g]
        qs = jnp.concatenate([jnp.where(m, qb, jnp.zeros_like(qb)) for m in masks], axis=0)
        sc = _dot_nt(qs, kb) + bias
        yield
        mx = jnp.max(sc, axis=-1, keepdims=True)
        p = jnp.exp2(sc - mx)
        l = jnp.sum(p, axis=-1, keepdims=True)
        yield
        o = _dot(p.astype(BF16), vb) / l
        lse = mx + jnp.log2(l)
        o_out = jnp.zeros((BLOCK, B_GROUP_WIDTH), F32)
        lse_out = jnp.zeros((BLOCK, B_GROUP_WIDTH), F32)
        for h, m in enumerate(masks):
            rows = slice(h * BLOCK, (h + 1) * BLOCK)
            o_out = jnp.where(m, o[rows], o_out)
            lse_out = jnp.where(m, lse[rows], lse_out)
        d = B_DILATIONS[g]
        idx = pl.ds(start, BLOCK) if d == 1 else pl.ds(start, BLOCK, stride=d)
        for j in range(SLABS):
            o_scr[g, j, idx, :] = o_out[:, j * LANES:(j + 1) * LANES]
            lse_scr[g, j, idx, :] = lse_out[:, j * LANES:(j + 1) * LANES]

    blocks = []
    for g, (q, k, v) in enumerate(((q0, k0, v0), (q1, k1, v1), (q2, k2, v2))):
        d = B_DILATIONS[g]
        for res in range(d):
            qr, kr, vr = (t.at[0] if d == 1 else t.at[0, res] for t in (q, k, v))
            blocks += [block(g, qr, kr, vr, n, res + n * (BLOCK * d)) for n in range(s // d // BLOCK)]
    _interleave(blocks)

    def merge_body(c, carry):
        rows = pl.ds(pl.multiple_of(c * MERGE_ROWS, MERGE_ROWS), MERGE_ROWS)
        for j in range(SLABS):
            l0, l1, l2 = lse_scr[0, j, rows, :], lse_scr[1, j, rows, :], lse_scr[2, j, rows, :]
            lm = jnp.maximum(jnp.maximum(l0, l1), l2)
            e0, e1, e2 = jnp.exp2(l0 - lm), jnp.exp2(l1 - lm), jnp.exp2(l2 - lm)
            y = (e0 * o_scr[0, j, rows, :] + e1 * o_scr[1, j, rows, :] + e2 * o_scr[2, j, rows, :]) / (e0 + e1 + e2)
            yb_ref[0, rows, j * LANES:(j + 1) * LANES] = y.astype(yb_ref.dtype)
        return carry

    lax.fori_loop(0, s // MERGE_ROWS, merge_body, 0)


def _dilated_attention(qkv, bias):
    bn, s, _ = qkv[0].shape
    in_specs = []
    for d in B_DILATIONS:
        if d == 1:
            spec = pl.BlockSpec((1, s, B_GROUP_WIDTH), lambda b: (b, 0, 0))
        else:
            spec = pl.BlockSpec((1, d, s // d, B_GROUP_WIDTH), lambda b: (b, 0, 0, 0))
        in_specs += [spec] * 3
    in_specs.append(_const_spec((B_GROUPS, B_HEADS_PER_GROUP * BLOCK, 2 * BLOCK)))
    return pl.pallas_call(
        functools.partial(_attn_kernel, s=s),
        grid=(bn,),
        in_specs=in_specs,
        out_specs=pl.BlockSpec((1, s, B_GROUP_WIDTH), lambda b: (b, 0, 0)),
        out_shape=jax.ShapeDtypeStruct((bn, s, B_GROUP_WIDTH), BF16),
        scratch_shapes=[pltpu.VMEM((B_GROUPS, SLABS, s, LANES), F32),
                        pltpu.VMEM((B_GROUPS, SLABS, s, LANES), F32)],
        compiler_params=pltpu.CompilerParams(
            dimension_semantics=("parallel",), vmem_limit_bytes=VMEM_LIMIT_BYTES),
        name="dilated_attn",
    )(*qkv, bias)


def _memkv_kernel(mem_ref, nw_ref, w_ref, kw_ref, mk_ref, mv_ref):
    rows = mem_ref.shape[0] * MEM_LEN
    hm = (_rms_rows(mem_ref[...].reshape(rows, D_MODEL)) * nw_ref[...]).astype(BF16)
    kv = _dot(hm, w_ref[...])
    for h in range(M_HEADS):
        cols = slice(h * M_HEAD_DIM, (h + 1) * M_HEAD_DIM)
        mk = (_rms_rows(kv[:, cols]) * kw_ref[...]).astype(BF16)
        mk_ref[:, :, cols] = mk.reshape(mem_ref.shape[0], MEM_LEN, M_HEAD_DIM)
    mv_ref[...] = kv[:, M_WIDTH:].astype(BF16).reshape(mv_ref.shape)


def _mem_kv(mem, mem_norm_w, w_kv, k_norm_w):
    bn = mem.shape[0]
    mb = math.gcd(bn, MEMKV_BATCH)
    out = jax.ShapeDtypeStruct((bn, MEM_LEN, M_WIDTH), BF16)
    ospec = pl.BlockSpec((mb, MEM_LEN, M_WIDTH), lambda b: (b, 0, 0))
    return pl.pallas_call(
        _memkv_kernel,
        grid=(bn // mb,),
        in_specs=[
            pl.BlockSpec((mb, MEM_LEN, D_MODEL), lambda b: (b, 0, 0)),
            _const_spec((1, D_MODEL)),
            _const_spec((D_MODEL, 2 * M_WIDTH)),
            _const_spec((1, M_HEAD_DIM)),
        ],
        out_specs=[ospec, ospec],
        out_shape=[out, out],
        compiler_params=pltpu.CompilerParams(dimension_semantics=("parallel",)),
        name="mem_kv",
    )(mem, mem_norm_w, w_kv, k_norm_w)


MW_AU, MW_AV, MW_AZ = 0, A_WIDTH, 2 * A_WIDTH
MW_BZ = OFF_BZ
MW_MQ = MW_BZ + B_GROUP_WIDTH
MW_MZ = MW_MQ + M_WIDTH
MW_G = MW_MZ + M_WIDTH
MW_TOTAL = MW_G + 3 * D_MODEL
MW_HI = MW_TOTAL // 2
assert MW_TOTAL % 2 == 0 and MW_HI % LANES == 0 and OFF_BQ <= MW_HI <= OFF_BZ


def _main_kernel(x_ref, yb_ref, mk_ref, mv_ref,
                 nw_ref, w_ref, wh_ref, gb_ref, lnw_ref, lnb_ref, sw_ref, sb_ref, mqw_ref,
                 pa_ref, pb_ref, pm_ref, wo_ref, out_ref, *, tm):
    ti = lax.broadcasted_iota(jnp.int32, (CHUNK, CHUNK), 0)
    si = lax.broadcasted_iota(jnp.int32, (CHUNK, CHUNK), 1)
    ws = [jnp.where(si <= ti, sw_ref[g], 0.0).astype(BF16) for g in range(A_GROUPS)]
    win_lo = [(g * A_GROUP_DIM) // LANES * LANES for g in range(A_GROUPS)]
    lane = lax.broadcasted_iota(jnp.int32, (1, LANES), 1)
    in_first = lane < (A_GROUP_DIM - LANES)

    def sub_tile(rows):
        n_rows = rows.stop - rows.start
        xf = x_ref[rows, :]
        h = (_rms_rows(xf) * nw_ref[...]).astype(BF16)

        def proj(lo, width):
            ref, base = (w_ref, 0) if lo < OFF_BQ else (wh_ref, MW_HI)
            return _dot(h, ref[:, lo - base:lo - base + width])

        def gate(i):
            return jax.nn.sigmoid(proj(MW_G + i * D_MODEL, D_MODEL) + gb_ref[i:i + 1, :])

        def mem_head(mq, hd):
            cols = slice(hd * M_HEAD_DIM, (hd + 1) * M_HEAD_DIM)
            qn = (_rms_rows(mq[:, cols]) * mqw_ref[...]).astype(BF16)
            sc = _dot_nt(qn, mk_ref[0, :, cols]) * (M_HEAD_DIM ** -0.5)
            sc = sc - jnp.max(sc, axis=-1, keepdims=True)
            p = jnp.exp(sc)
            l = jnp.sum(p, axis=-1, keepdims=True)
            return _dot(p.astype(BF16), mv_ref[0, :, cols]) / l

        a_u = proj(MW_AU, A_WIDTH)
        a_v = proj(MW_AV, A_WIDTH)
        yield
        u = _gelu_exact(a_u)
        gv = _gelu_exact(a_v)
        yield
        mq = proj(MW_MQ, M_WIDTH)
        a_z = proj(MW_AZ, A_WIDTH)
        yield
        mu = jnp.mean(gv, axis=-1, keepdims=True)
        gc = gv - mu
        vv = gc * lax.rsqrt(jnp.mean(gc * gc, axis=-1, keepdims=True) + EPS) * lnw_ref[...] + lnb_ref[...]
        vv = vv.astype(BF16)
        mixed_chunks = []
        for c in range(n_rows // CHUNK):
            vc = vv[c * CHUNK:(c + 1) * CHUNK, :]
            m = [_dot(ws[g], vc[:, win_lo[g]:win_lo[g] + 2 * LANES]) for g in range(A_GROUPS)]
            lo, hi = (lambda t: t[:, :LANES]), (lambda t: t[:, LANES:])
            pieces = [lo(m[0]), jnp.where(in_first, hi(m[0]), lo(m[1])), hi(m[1]),
                      lo(m[2]), jnp.where(in_first, hi(m[2]), lo(m[3])), hi(m[3])]
            mixed_chunks.append(jnp.concatenate(pieces, axis=-1) + sb_ref[...])
        mixed = jnp.concatenate(mixed_chunks, axis=0)
        y_a = (u * mixed * jax.nn.silu(a_z)).astype(BF16)
        yield
        acc = gate(0) * _dot(y_a, pa_ref[...])
        yield

        ym_heads = [mem_head(mq, hd) for hd in range(M_HEADS)]
        yield
        m_z = proj(MW_MZ, M_WIDTH)
        b_z = proj(MW_BZ, B_GROUP_WIDTH)
        yield
        y_m = (jnp.concatenate(ym_heads, axis=-1) * jax.nn.silu(m_z)).astype(BF16)
        y_b = (yb_ref[rows, :].astype(F32) * jax.nn.silu(b_z)).astype(BF16)
        yield
        acc = acc + gate(2) * _dot(y_m, pm_ref[...])
        yield
        acc = acc + gate(1) * _dot(y_b, pb_ref[...])
        yield
        out_ref[rows, :] = xf + _dot(acc.astype(BF16), wo_ref[...])

    rows_per = tm // MAIN_SUBTILES
    _interleave([sub_tile(slice(k * rows_per, (k + 1) * rows_per)) for k in range(MAIN_SUBTILES)])


def _main_block(x2, yb, mk, mv, norm_w, w_main, gate_b, lnw, lnb, sw, sb_full, mqw,
                proj_a, proj_b, proj_m, w_out, s):
    n = x2.shape[0]
    tm = MAIN_TM
    steps_per_batch = s // tm
    tile = lambda width: pl.BlockSpec((tm, width), lambda i: (i, 0))
    mem_spec = pl.BlockSpec((1, MEM_LEN, M_WIDTH), lambda i: (i // steps_per_batch, 0, 0))
    return pl.pallas_call(
        functools.partial(_main_kernel, tm=tm),
        grid=(n // tm,),
        in_specs=[tile(D_MODEL), tile(B_GROUP_WIDTH), mem_spec, mem_spec,
                  _const_spec((1, D_MODEL)),
                  pl.BlockSpec((D_MODEL, OFF_BQ), lambda i: (0, 0), pipeline_mode=pl.Buffered(1)),
                  pl.BlockSpec((D_MODEL, MW_HI), lambda i: (0, 1), pipeline_mode=pl.Buffered(1)),
                  _const_spec((3, D_MODEL)),
                  _const_spec((1, A_WIDTH)),
                  _const_spec((1, A_WIDTH)),
                  _const_spec((A_GROUPS, CHUNK, CHUNK)),
                  _const_spec((CHUNK, A_WIDTH)),
                  _const_spec((1, M_HEAD_DIM)),
                  _const_spec((A_WIDTH, D_MODEL)),
                  _const_spec((B_GROUP_WIDTH, D_MODEL)),
                  _const_spec((M_WIDTH, D_MODEL)),
                  _const_spec((D_MODEL, D_MODEL))],
        out_specs=tile(D_MODEL),
        out_shape=jax.ShapeDtypeStruct((n, D_MODEL), F32),
        compiler_params=pltpu.CompilerParams(
            dimension_semantics=("parallel",), vmem_limit_bytes=VMEM_LIMIT_BYTES),
        name="main_block",
    )(x2, yb, mk, mv, norm_w, w_main, w_main, gate_b, lnw, lnb, sw, sb_full, mqw,
      proj_a, proj_b, proj_m, w_out)


def kernel(x, mem, norm_w, w_in, gate_b, a_v_norm_w, a_v_norm_b, a_spatial_w, a_spatial_b,
           b_q_norm_w, b_k_norm_w, rel_bias, mem_norm_w, m_w_kv, m_q_norm_w, m_k_norm_w,
           proj_a, proj_b, proj_m, w_out):
    bn, s, _ = x.shape
    n = bn * s
    row = lambda v: v.reshape(1, -1)

    w_bf = w_in.astype(BF16)
    wq_t = row(jnp.tile(b_q_norm_w * (B_HEAD_DIM ** -0.5 * LOG2E), B_HEADS_PER_GROUP))
    wk_t = row(jnp.tile(b_k_norm_w, B_HEADS_PER_GROUP))
    sb_full = jnp.repeat(a_spatial_b.T, A_GROUP_DIM, axis=1)

    bias = _expand_bias(rel_bias)
    qkv = _qkv_proj(x, row(norm_w), w_bf, wq_t, wk_t)
    yb = _dilated_attention(qkv, bias)
    mk, mv = _mem_kv(mem, row(mem_norm_w), m_w_kv.astype(BF16), row(m_k_norm_w))
    out = _main_block(x.reshape(n, D_MODEL), yb.reshape(n, B_GROUP_WIDTH), mk, mv, row(norm_w), w_bf, gate_b,
                      row(a_v_norm_w), row(a_v_norm_b), a_spatial_w, sb_full, row(m_q_norm_w),
                      proj_a.astype(BF16), proj_b.astype(BF16), proj_m.astype(BF16), w_out.astype(BF16), s)
    return out.reshape(bn, s, D_MODEL)
```

```python
import functools
import math

import jax
import jax.numpy as jnp
import numpy as np
from jax import lax
from jax.experimental import pallas as pl
from jax.experimental.pallas import tpu as pltpu

EPS = 1e-6
D_MODEL = 1024
A_WIDTH = 768
A_GROUPS = 4
A_GROUP_DIM = A_WIDTH // A_GROUPS
CHUNK = 128
B_PATTERNS = ((128, 1), (512, 4), (2048, 16))
B_DILATIONS = tuple(d for _, d in B_PATTERNS)
B_GROUPS = 3
B_HEADS_PER_GROUP = 4
B_HEAD_DIM = 64
B_GROUP_WIDTH = B_HEADS_PER_GROUP * B_HEAD_DIM
B_QKV_WIDTH = B_GROUPS * B_GROUP_WIDTH
BLOCK = 128
MEM_LEN = 256
M_HEADS = 4
M_HEAD_DIM = 128
M_WIDTH = M_HEADS * M_HEAD_DIM
REL_BUCKETS = 32
REL_MAX_DISTANCE = 2048
NEG = -1e30
LOG2E = math.log2(math.e)

OFF_BQ = 3 * A_WIDTH
OFF_BZ = OFF_BQ + 3 * B_QKV_WIDTH

LANES = 128
SLABS = B_GROUP_WIDTH // LANES
VMEM_LIMIT_BYTES = 60 * 1024 * 1024

QKV_TM = 1024
MAIN_TM = 1024
MERGE_ROWS = 256
MEMKV_BATCH = 4
MAIN_SUBTILES = 2
QKV_SUBTILES = 2
assert OFF_BQ % (OFF_BZ - OFF_BQ) == 0

BF16 = jnp.bfloat16
F32 = jnp.float32


def _dot(a, b):
    return jnp.dot(a, b, preferred_element_type=F32)


def _dot_nt(a, b):
    return lax.dot_general(a, b, (((1,), (1,)), ((), ())), preferred_element_type=F32)


def _rms_rows(xf):
    return xf * lax.rsqrt(jnp.mean(xf * xf, axis=-1, keepdims=True) + EPS)


def _gelu_exact(t):
    return 0.5 * t * (1.0 + lax.erf(t * (2.0 ** -0.5)))


def _interleave(stage_generators):
    done = object()
    queue, live = list(stage_generators), []
    while queue or live:
        if queue:
            live.append(queue.pop(0))
        live = [gen for gen in live if next(gen, done) is not done]


def _const_spec(shape):
    nd = len(shape)
    return pl.BlockSpec(shape, lambda *_: (0,) * nd, pipeline_mode=pl.Buffered(1))


def _t5_causal_bucket(dist):
    max_exact = REL_BUCKETS // 2
    df = np.maximum(dist, 1).astype(np.float32)
    scaled = (np.log(df / np.float32(max_exact)) / np.float32(math.log(REL_MAX_DISTANCE / max_exact))
              * np.float32(REL_BUCKETS - max_exact))
    large = np.minimum(max_exact + scaled.astype(np.int32), REL_BUCKETS - 1)
    return np.where(dist < max_exact, dist, large).astype(np.int32)


def _bias_kernel(bucket_ref, valid_ref, rb_ref, out_ref):
    g = pl.program_id(0)
    bk = bucket_ref[0]
    valid = valid_ref[...] != 0
    for h in range(B_HEADS_PER_GROUP):
        acc = jnp.zeros((BLOCK, 2 * BLOCK), F32)
        for b in range(REL_BUCKETS):
            acc = jnp.where(bk == b, rb_ref[b, g * B_HEADS_PER_GROUP + h], acc)
        out_ref[0, h * BLOCK:(h + 1) * BLOCK, :] = jnp.where(valid, acc * LOG2E, NEG)


def _expand_bias(rel_bias):
    qi = np.arange(BLOCK, dtype=np.int32)[:, None]
    kj = np.arange(2 * BLOCK, dtype=np.int32)[None, :]
    step = qi + BLOCK - kj
    buckets = np.stack([_t5_causal_bucket(np.maximum(step, 0) * d) for d in B_DILATIONS])
    valid = ((step >= 0) & (step <= BLOCK)).astype(np.int32)
    return pl.pallas_call(
        _bias_kernel,
        grid=(B_GROUPS,),
        in_specs=[
            pl.BlockSpec((1, BLOCK, 2 * BLOCK), lambda g: (g, 0, 0)),
            pl.BlockSpec((BLOCK, 2 * BLOCK), lambda g: (0, 0)),
            pl.BlockSpec(memory_space=pltpu.SMEM),
        ],
        out_specs=pl.BlockSpec((1, B_HEADS_PER_GROUP * BLOCK, 2 * BLOCK), lambda g: (g, 0, 0)),
        out_shape=jax.ShapeDtypeStruct((B_GROUPS, B_HEADS_PER_GROUP * BLOCK, 2 * BLOCK), F32),
        name="rel_bias_expand",
    )(buckets, valid, rel_bias)


def _qkv_kernel(x_ref, nw_ref, w_ref, wq_ref, wk_ref, *refs, tm):
    out_refs = refs[:3 * B_GROUPS]
    scratch = refs[3 * B_GROUPS:]
    n_slabs = 3 * (B_GROUPS - 1)
    rows = tm // QKV_SUBTILES
    r = lax.broadcasted_iota(jnp.int32, (B_GROUP_WIDTH, B_GROUP_WIDTH), 0) // B_HEAD_DIM
    c = lax.broadcasted_iota(jnp.int32, (B_GROUP_WIDTH, B_GROUP_WIDTH), 1) // B_HEAD_DIM
    seg = jnp.where(r == c, 1.0 / B_HEAD_DIM, 0.0).astype(BF16)

    def head_norm(t, w):
        ms = _dot((t * t).astype(BF16), seg)
        return t * lax.rsqrt(ms + EPS) * w

    def sub_tile(k):
        h = (_rms_rows(x_ref[0, k * rows:(k + 1) * rows, :]) * nw_ref[...]).astype(BF16)

        def emit(g, which, t):
            out = out_refs[3 * g + which]
            d = B_DILATIONS[g]
            sub = rows // d
            if d == 1:
                out[0, k * rows:(k + 1) * rows, :] = t.astype(BF16)
                return
            scr = scratch[k * n_slabs + 3 * (g - 1) + which]
            for j in range(SLABS):
                scr[j] = t[:, j * LANES:(j + 1) * LANES]
            for res in range(d):
                for j in range(SLABS):
                    out[0, res, k * sub:(k + 1) * sub, j * LANES:(j + 1) * LANES] = (
                        scr[j, pl.ds(res, sub, stride=d), :].astype(BF16))

        def piece(g, which):
            lo = which * B_QKV_WIDTH + g * B_GROUP_WIDTH
            return _dot(h, w_ref[:, lo:lo + B_GROUP_WIDTH])

        qk = [[piece(g, which) for which in range(2)] for g in range(B_GROUPS)]
        yield
        for g in reversed(range(B_GROUPS)):
            emit(g, 0, head_norm(qk[g][0], wq_ref[...]))
            emit(g, 1, head_norm(qk[g][1], wk_ref[...]))
            emit(g, 2, piece(g, 2))
            yield

    _interleave([sub_tile(k) for k in range(QKV_SUBTILES)])


def _qkv_proj(x, norm_w, w_qkv, wq_t, wk_t):
    bn, s, _ = x.shape
    tm = QKV_TM
    out_shapes, out_specs, scratch = [], [], []
    for d in B_DILATIONS:
        if d == 1:
            shape = jax.ShapeDtypeStruct((bn, s, B_GROUP_WIDTH), BF16)
            spec = pl.BlockSpec((1, tm, B_GROUP_WIDTH), lambda b, j: (b, j, 0))
        else:
            shape = jax.ShapeDtypeStruct((bn, d, s // d, B_GROUP_WIDTH), BF16)
            spec = pl.BlockSpec((1, d, tm // d, B_GROUP_WIDTH), lambda b, j: (b, 0, j, 0))
            scratch += [pltpu.VMEM((SLABS, tm // QKV_SUBTILES, LANES), F32)] * 3
        out_shapes += [shape] * 3
        out_specs += [spec] * 3
    return pl.pallas_call(
        functools.partial(_qkv_kernel, tm=tm),
        grid=(bn, s // tm),
        in_specs=[
            pl.BlockSpec((1, tm, D_MODEL), lambda b, j: (b, j, 0)),
            _const_spec((1, D_MODEL)),
            pl.BlockSpec((D_MODEL, OFF_BZ - OFF_BQ), lambda b, j: (0, OFF_BQ // (OFF_BZ - OFF_BQ)),
                         pipeline_mode=pl.Buffered(1)),
            _const_spec((1, B_GROUP_WIDTH)),
            _const_spec((1, B_GROUP_WIDTH)),
        ],
        out_specs=out_specs,
        out_shape=out_shapes,
        scratch_shapes=scratch * QKV_SUBTILES,
        compiler_params=pltpu.CompilerParams(
            dimension_semantics=("parallel", "parallel"), vmem_limit_bytes=VMEM_LIMIT_BYTES),
        name="qkv_proj",
    )(x, norm_w, w_qkv, wq_t, wk_t)


def _attn_kernel(q0, k0, v0, q1, k1, v1, q2, k2, v2, bias_ref, yb_ref, o_scr, lse_scr, *, s):
    lane_head = lax.broadcasted_iota(jnp.int32, (1, B_GROUP_WIDTH), 1) // B_HEAD_DIM
    masks = [lane_head == h for h in range(B_HEADS_PER_GROUP)]

    def block(g, q, k, v, n, start):
        qb = q[n * BLOCK:(n + 1) * BLOCK, :]
        if n == 0:
            kb, vb = k[0:BLOCK, :], v[0:BLOCK, :]
            bias = bias_ref[g, :, BLOCK:2 * BLOCK]
        else:
            kb, vb = k[(n - 1) * BLOCK:(n + 1) * BLOCK, :], v[(n - 1) * BLOCK:(n + 1) * BLOCK, :]
            bias = bias_ref[g]
        qs = jnp.concatenate([jnp.where(m, qb, jnp.zeros_like(qb)) for m in masks], axis=0)
        sc = _dot_nt(qs, kb) + bias
        yield
        mx = jnp.max(sc, axis=-1, keepdims=True)
        p = jnp.exp2(sc - mx)
        l = jnp.sum(p, axis=-1, keepdims=True)
        yield
        o = _dot(p.astype(BF16), vb) / l
        lse = mx + jnp.log2(l)
        o_out = jnp.zeros((BLOCK, B_GROUP_WIDTH), F32)
        lse_out = jnp.zeros((BLOCK, B_GROUP_WIDTH), F32)
        for h, m in enumerate(masks):
            rows = slice(h * BLOCK, (h + 1) * BLOCK)
            o_out = jnp.where(m, o[rows], o_out)
            lse_out = jnp.where(m, lse[rows], lse_out)
        d = B_DILATIONS[g]
        idx = pl.ds(start, BLOCK) if d == 1 else pl.ds(start, BLOCK, stride=d)
        for j in range(SLABS):
            o_scr[g, j, idx, :] = o_out[:, j * LANES:(j + 1) * LANES]
            lse_scr[g, j, idx, :] = lse_out[:, j * LANES:(j + 1) * LANES]

    blocks = []
    for g, (q, k, v) in enumerate(((q0, k0, v0), (q1, k1, v1), (q2, k2, v2))):
        d = B_DILATIONS[g]
        for res in range(d):
            qr, kr, vr = (t.at[0] if d == 1 else t.at[0, res] for t in (q, k, v))
            blocks += [block(g, qr, kr, vr, n, res + n * (BLOCK * d)) for n in range(s // d // BLOCK)]
    _interleave(blocks)

    def merge_body(c, carry):
        rows = pl.ds(pl.multiple_of(c * MERGE_ROWS, MERGE_ROWS), MERGE_ROWS)
        for j in range(SLABS):
            l0, l1, l2 = lse_scr[0, j, rows, :], lse_scr[1, j, rows, :], lse_scr[2, j, rows, :]
            lm = jnp.maximum(jnp.maximum(l0, l1), l2)
            e0, e1, e2 = jnp.exp2(l0 - lm), jnp.exp2(l1 - lm), jnp.exp2(l2 - lm)
            y = (e0 * o_scr[0, j, rows, :] + e1 * o_scr[1, j, rows, :] + e2 * o_scr[2, j, rows, :]) / (e0 + e1 + e2)
            yb_ref[0, rows, j * LANES:(j + 1) * LANES] = y.astype(yb_ref.dtype)
        return carry

    lax.fori_loop(0, s // MERGE_ROWS, merge_body, 0)


def _dilated_attention(qkv, bias):
    bn, s, _ = qkv[0].shape
    in_specs = []
    for d in B_DILATIONS:
        if d == 1:
            spec = pl.BlockSpec((1, s, B_GROUP_WIDTH), lambda b: (b, 0, 0))
        else:
            spec = pl.BlockSpec((1, d, s // d, B_GROUP_WIDTH), lambda b: (b, 0, 0, 0))
        in_specs += [spec] * 3
    in_specs.append(_const_spec((B_GROUPS, B_HEADS_PER_GROUP * BLOCK, 2 * BLOCK)))
    return pl.pallas_call(
        functools.partial(_attn_kernel, s=s),
        grid=(bn,),
        in_specs=in_specs,
        out_specs=pl.BlockSpec((1, s, B_GROUP_WIDTH), lambda b: (b, 0, 0)),
        out_shape=jax.ShapeDtypeStruct((bn, s, B_GROUP_WIDTH), BF16),
        scratch_shapes=[pltpu.VMEM((B_GROUPS, SLABS, s, LANES), F32),
                        pltpu.VMEM((B_GROUPS, SLABS, s, LANES), F32)],
        compiler_params=pltpu.CompilerParams(
            dimension_semantics=("parallel",), vmem_limit_bytes=VMEM_LIMIT_BYTES),
        name="dilated_attn",
    )(*qkv, bias)


def _memkv_kernel(mem_ref, nw_ref, w_ref, kw_ref, mk_ref, mv_ref):
    rows = mem_ref.shape[0] * MEM_LEN
    hm = (_rms_rows(mem_ref[...].reshape(rows, D_MODEL)) * nw_ref[...]).astype(BF16)
    kv = _dot(hm, w_ref[...])
    for h in range(M_HEADS):
        cols = slice(h * M_HEAD_DIM, (h + 1) * M_HEAD_DIM)
        mk = (_rms_rows(kv[:, cols]) * kw_ref[...]).astype(BF16)
        mk_ref[:, :, cols] = mk.reshape(mem_ref.shape[0], MEM_LEN, M_HEAD_DIM)
    mv_ref[...] = kv[:, M_WIDTH:].astype(BF16).reshape(mv_ref.shape)


def _mem_kv(mem, mem_norm_w, w_kv, k_norm_w):
    bn = mem.shape[0]
    mb = math.gcd(bn, MEMKV_BATCH)
    out = jax.ShapeDtypeStruct((bn, MEM_LEN, M_WIDTH), BF16)
    ospec = pl.BlockSpec((mb, MEM_LEN, M_WIDTH), lambda b: (b, 0, 0))
    return pl.pallas_call(
        _memkv_kernel,
        grid=(bn // mb,),
        in_specs=[
            pl.BlockSpec((mb, MEM_LEN, D_MODEL), lambda b: (b, 0, 0)),
            _const_spec((1, D_MODEL)),
            _const_spec((D_MODEL, 2 * M_WIDTH)),
            _const_spec((1, M_HEAD_DIM)),
        ],
        out_specs=[ospec, ospec],
        out_shape=[out, out],
        compiler_params=pltpu.CompilerParams(dimension_semantics=("parallel",)),
        name="mem_kv",
    )(mem, mem_norm_w, w_kv, k_norm_w)


MW_AU, MW_AV, MW_AZ = 0, A_WIDTH, 2 * A_WIDTH
MW_BZ = OFF_BZ
MW_MQ = MW_BZ + B_GROUP_WIDTH
MW_MZ = MW_MQ + M_WIDTH
MW_G = MW_MZ + M_WIDTH
MW_TOTAL = MW_G + 3 * D_MODEL
MW_HI = MW_TOTAL // 2
assert MW_TOTAL % 2 == 0 and MW_HI % LANES == 0 and OFF_BQ <= MW_HI <= OFF_BZ


def _main_kernel(x_ref, yb_ref, mk_ref, mv_ref,
                 nw_ref, w_ref, wh_ref, gb_ref, lnw_ref, lnb_ref, sw_ref, sb_ref, mqw_ref,
                 pa_ref, pb_ref, pm_ref, wo_ref, out_ref, *, tm):
    ti = lax.broadcasted_iota(jnp.int32, (CHUNK, CHUNK), 0)
    si = lax.broadcasted_iota(jnp.int32, (CHUNK, CHUNK), 1)
    ws = [jnp.where(si <= ti, sw_ref[g], 0.0).astype(BF16) for g in range(A_GROUPS)]
    win_lo = [(g * A_GROUP_DIM) // LANES * LANES for g in range(A_GROUPS)]
    lane = lax.broadcasted_iota(jnp.int32, (1, LANES), 1)
    in_first = lane < (A_GROUP_DIM - LANES)

    def sub_tile(rows):
        n_rows = rows.stop - rows.start
        xf = x_ref[rows, :]
        h = (_rms_rows(xf) * nw_ref[...]).astype(BF16)

        def proj(lo, width):
            ref, base = (w_ref, 0) if lo < OFF_BQ else (wh_ref, MW_HI)
            return _dot(h, ref[:, lo - base:lo - base + width])

        def gate(i):
            return jax.nn.sigmoid(proj(MW_G + i * D_MODEL, D_MODEL) + gb_ref[i:i + 1, :])

        def mem_head(mq, hd):
            cols = slice(hd * M_HEAD_DIM, (hd + 1) * M_HEAD_DIM)
            qn = (_rms_rows(mq[:, cols]) * mqw_ref[...]).astype(BF16)
            sc = _dot_nt(qn, mk_ref[0, :, cols]) * (M_HEAD_DIM ** -0.5)
            sc = sc - jnp.max(sc, axis=-1, keepdims=True)
            p = jnp.exp(sc)
            l = jnp.sum(p, axis=-1, keepdims=True)
            return _dot(p.astype(BF16), mv_ref[0, :, cols]) / l

        a_u = proj(MW_AU, A_WIDTH)
        a_v = proj(MW_AV, A_WIDTH)
        yield
        u = _gelu_exact(a_u)
        yield
        mq = proj(MW_MQ, M_WIDTH)
        a_z = proj(MW_AZ, A_WIDTH)
        yield
        gv = _gelu_exact(a_v)
        yield
        m_z = proj(MW_MZ, M_WIDTH)
        b_z = proj(MW_BZ, B_GROUP_WIDTH)
        yield
        mu = jnp.mean(gv, axis=-1, keepdims=True)
        gc = gv - mu
        vv = gc * lax.rsqrt(jnp.mean(gc * gc, axis=-1, keepdims=True) + EPS) * lnw_ref[...] + lnb_ref[...]
        vv = vv.astype(BF16)
        mixed_chunks = []
        for c in range(n_rows // CHUNK):
            vc = vv[c * CHUNK:(c + 1) * CHUNK, :]
            m = [_dot(ws[g], vc[:, win_lo[g]:win_lo[g] + 2 * LANES]) for g in range(A_GROUPS)]
            lo, hi = (lambda t: t[:, :LANES]), (lambda t: t[:, LANES:])
            pieces = [lo(m[0]), jnp.where(in_first, hi(m[0]), lo(m[1])), hi(m[1]),
                      lo(m[2]), jnp.where(in_first, hi(m[2]), lo(m[3])), hi(m[3])]
            mixed_chunks.append(jnp.concatenate(pieces, axis=-1) + sb_ref[...])
        mixed = jnp.concatenate(mixed_chunks, axis=0)
        y_a = (u * mixed * jax.nn.silu(a_z)).astype(BF16)
        yield
        acc = gate(0) * _dot(y_a, pa_ref[...])
        yield

        ym_heads = [mem_head(mq, hd) for hd in range(M_HEADS)]
        y_m = (jnp.concatenate(ym_heads, axis=-1) * jax.nn.silu(m_z)).astype(BF16)
        y_b = (yb_ref[rows, :].astype(F32) * jax.nn.silu(b_z)).astype(BF16)
        yield
        acc = acc + gate(2) * _dot(y_m, pm_ref[...])
        yield
        acc = acc + gate(1) * _dot(y_b, pb_ref[...])
        yield
        out_ref[rows, :] = xf + _dot(acc.astype(BF16), wo_ref[...])

    rows_per = tm // MAIN_SUBTILES
    _interleave([sub_tile(slice(k * rows_per, (k + 1) * rows_per)) for k in range(MAIN_SUBTILES)])


def _main_block(x2, yb, mk, mv, norm_w, w_main, gate_b, lnw, lnb, sw, sb_full, mqw,
                proj_a, proj_b, proj_m, w_out, s):
    n = x2.shape[0]
    tm = MAIN_TM
    steps_per_batch = s // tm
    tile = lambda width: pl.BlockSpec((tm, width), lambda i: (i, 0))
    mem_spec = pl.BlockSpec((1, MEM_LEN, M_WIDTH), lambda i: (i // steps_per_batch, 0, 0))
    return pl.pallas_call(
        functools.partial(_main_kernel, tm=tm),
        grid=(n // tm,),
        in_specs=[tile(D_MODEL), tile(B_GROUP_WIDTH), mem_spec, mem_spec,
                  _const_spec((1, D_MODEL)),
                  pl.BlockSpec((D_MODEL, OFF_BQ), lambda i: (0, 0), pipeline_mode=pl.Buffered(1)),
                  pl.BlockSpec((D_MODEL, MW_HI), lambda i: (0, 1), pipeline_mode=pl.Buffered(1)),
                  _const_spec((3, D_MODEL)),
                  _const_spec((1, A_WIDTH)),
                  _const_spec((1, A_WIDTH)),
                  _const_spec((A_GROUPS, CHUNK, CHUNK)),
                  _const_spec((CHUNK, A_WIDTH)),
                  _const_spec((1, M_HEAD_DIM)),
                  _const_spec((A_WIDTH, D_MODEL)),
                  _const_spec((B_GROUP_WIDTH, D_MODEL)),
                  _const_spec((M_WIDTH, D_MODEL)),
                  _const_spec((D_MODEL, D_MODEL))],
        out_specs=tile(D_MODEL),
        out_shape=jax.ShapeDtypeStruct((n, D_MODEL), F32),
        compiler_params=pltpu.CompilerParams(
            dimension_semantics=("parallel",), vmem_limit_bytes=VMEM_LIMIT_BYTES),
        name="main_block",
    )(x2, yb, mk, mv, norm_w, w_main, w_main, gate_b, lnw, lnb, sw, sb_full, mqw,
      proj_a, proj_b, proj_m, w_out)


def kernel(x, mem, norm_w, w_in, gate_b, a_v_norm_w, a_v_norm_b, a_spatial_w, a_spatial_b,
           b_q_norm_w, b_k_norm_w, rel_bias, mem_norm_w, m_w_kv, m_q_norm_w, m_k_norm_w,
           proj_a, proj_b, proj_m, w_out):
    bn, s, _ = x.shape
    n = bn * s
    row = lambda v: v.reshape(1, -1)

    w_bf = w_in.astype(BF16)
    wq_t = row(jnp.tile(b_q_norm_w * (B_HEAD_DIM ** -0.5 * LOG2E), B_HEADS_PER_GROUP))
    wk_t = row(jnp.tile(b_k_norm_w, B_HEADS_PER_GROUP))
    sb_full = jnp.repeat(a_spatial_b.T, A_GROUP_DIM, axis=1)

    bias = _expand_bias(rel_bias)
    qkv = _qkv_proj(x, row(norm_w), w_bf, wq_t, wk_t)
    yb = _dilated_attention(qkv, bias)
    mk, mv = _mem_kv(mem, row(mem_norm_w), m_w_kv.astype(BF16), row(m_k_norm_w))
    out = _main_block(x.reshape(n, D_MODEL), yb.reshape(n, B_GROUP_WIDTH), mk, mv, row(norm_w), w_bf, gate_b,
                      row(a_v_norm_w), row(a_v_norm_b), a_spatial_w, sb_full, row(m_q_norm_w),
                      proj_a.astype(BF16), proj_b.astype(BF16), proj_m.astype(BF16), w_out.astype(BF16), s)
    return out.reshape(bn, s, D_MODEL)
```

```python
import functools
import math

import jax
import jax.numpy as jnp
import numpy as np
from jax import lax
from jax.experimental import pallas as pl
from jax.experimental.pallas import tpu as pltpu

EPS = 1e-6
D_MODEL = 1024
A_WIDTH = 768
A_GROUPS = 4
A_GROUP_DIM = A_WIDTH // A_GROUPS
CHUNK = 128
B_PATTERNS = ((128, 1), (512, 4), (2048, 16))
B_DILATIONS = tuple(d for _, d in B_PATTERNS)
B_GROUPS = 3
B_HEADS_PER_GROUP = 4
B_HEAD_DIM = 64
B_GROUP_WIDTH = B_HEADS_PER_GROUP * B_HEAD_DIM
B_QKV_WIDTH = B_GROUPS * B_GROUP_WIDTH
BLOCK = 128
MEM_LEN = 256
M_HEADS = 4
M_HEAD_DIM = 128
M_WIDTH = M_HEADS * M_HEAD_DIM
REL_BUCKETS = 32
REL_MAX_DISTANCE = 2048
NEG = -1e30
LOG2E = math.log2(math.e)

OFF_BQ = 3 * A_WIDTH
OFF_BZ = OFF_BQ + 3 * B_QKV_WIDTH

LANES = 128
SLABS = B_GROUP_WIDTH // LANES
VMEM_LIMIT_BYTES = 60 * 1024 * 1024

QKV_TM = 1024
MAIN_TM = 1024
MERGE_ROWS = 256
MEMKV_BATCH = 4
MAIN_SUBTILES = 2
QKV_SUBTILES = 2
assert OFF_BQ % (OFF_BZ - OFF_BQ) == 0

BF16 = jnp.bfloat16
F32 = jnp.float32


def _dot(a, b):
    return jnp.dot(a, b, preferred_element_type=F32)


def _dot_nt(a, b):
    return lax.dot_general(a, b, (((1,), (1,)), ((), ())), preferred_element_type=F32)


def _rms_rows(xf):
    return xf * lax.rsqrt(jnp.mean(xf * xf, axis=-1, keepdims=True) + EPS)


def _gelu_exact(t):
    return 0.5 * t * (1.0 + lax.erf(t * (2.0 ** -0.5)))


def _interleave(stage_generators):
    done = object()
    queue, live = list(stage_generators), []
    while queue or live:
        if queue:
            live.append(queue.pop(0))
        live = [gen for gen in live if next(gen, done) is not done]


def _const_spec(shape):
    nd = len(shape)
    return pl.BlockSpec(shape, lambda *_: (0,) * nd, pipeline_mode=pl.Buffered(1))


def _t5_causal_bucket(dist):
    max_exact = REL_BUCKETS // 2
    df = np.maximum(dist, 1).astype(np.float32)
    scaled = (np.log(df / np.float32(max_exact)) / np.float32(math.log(REL_MAX_DISTANCE / max_exact))
              * np.float32(REL_BUCKETS - max_exact))
    large = np.minimum(max_exact + scaled.astype(np.int32), REL_BUCKETS - 1)
    return np.where(dist < max_exact, dist, large).astype(np.int32)


def _bias_kernel(bucket_ref, valid_ref, rb_ref, out_ref):
    g = pl.program_id(0)
    bk = bucket_ref[0]
    valid = valid_ref[...] != 0
    for h in range(B_HEADS_PER_GROUP):
        acc = jnp.zeros((BLOCK, 2 * BLOCK), F32)
        for b in range(REL_BUCKETS):
            acc = jnp.where(bk == b, rb_ref[b, g * B_HEADS_PER_GROUP + h], acc)
        out_ref[0, h * BLOCK:(h + 1) * BLOCK, :] = jnp.where(valid, acc * LOG2E, NEG)


def _expand_bias(rel_bias):
    qi = np.arange(BLOCK, dtype=np.int32)[:, None]
    kj = np.arange(2 * BLOCK, dtype=np.int32)[None, :]
    step = qi + BLOCK - kj
    buckets = np.stack([_t5_causal_bucket(np.maximum(step, 0) * d) for d in B_DILATIONS])
    valid = ((step >= 0) & (step <= BLOCK)).astype(np.int32)
    return pl.pallas_call(
        _bias_kernel,
        grid=(B_GROUPS,),
        in_specs=[
            pl.BlockSpec((1, BLOCK, 2 * BLOCK), lambda g: (g, 0, 0)),
            pl.BlockSpec((BLOCK, 2 * BLOCK), lambda g: (0, 0)),
            pl.BlockSpec(memory_space=pltpu.SMEM),
        ],
        out_specs=pl.BlockSpec((1, B_HEADS_PER_GROUP * BLOCK, 2 * BLOCK), lambda g: (g, 0, 0)),
        out_shape=jax.ShapeDtypeStruct((B_GROUPS, B_HEADS_PER_GROUP * BLOCK, 2 * BLOCK), F32),
        name="rel_bias_expand",
    )(buckets, valid, rel_bias)


def _qkv_kernel(x_ref, nw_ref, w_ref, wq_ref, wk_ref, *refs, tm):
    out_refs = refs[:3 * B_GROUPS]
    scratch = refs[3 * B_GROUPS:]
    n_slabs = 3 * (B_GROUPS - 1)
    rows = tm // QKV_SUBTILES
    r = lax.broadcasted_iota(jnp.int32, (B_GROUP_WIDTH, B_GROUP_WIDTH), 0) // B_HEAD_DIM
    c = lax.broadcasted_iota(jnp.int32, (B_GROUP_WIDTH, B_GROUP_WIDTH), 1) // B_HEAD_DIM
    seg = jnp.where(r == c, 1.0 / B_HEAD_DIM, 0.0).astype(BF16)

    def head_norm(t, w):
        ms = _dot((t * t).astype(BF16), seg)
        return t * lax.rsqrt(ms + EPS) * w

    def sub_tile(k):
        h = (_rms_rows(x_ref[0, k * rows:(k + 1) * rows, :]) * nw_ref[...]).astype(BF16)

        def emit(g, which, t):
            out = out_refs[3 * g + which]
            d = B_DILATIONS[g]
            sub = rows // d
            if d == 1:
                out[0, k * rows:(k + 1) * rows, :] = t.astype(BF16)
                return
            scr = scratch[k * n_slabs + 3 * (g - 1) + which]
            for j in range(SLABS):
                scr[j] = t[:, j * LANES:(j + 1) * LANES]
            for res in range(d):
                for j in range(SLABS):
                    out[0, res, k * sub:(k + 1) * sub, j * LANES:(j + 1) * LANES] = (
                        scr[j, pl.ds(res, sub, stride=d), :].astype(BF16))

        def piece(g, which):
            lo = which * B_QKV_WIDTH + g * B_GROUP_WIDTH
            return _dot(h, w_ref[:, lo:lo + B_GROUP_WIDTH])

        qk = [[piece(g, which) for which in range(2)] for g in range(B_GROUPS)]
        yield
        for g in reversed(range(B_GROUPS)):
            emit(g, 0, head_norm(qk[g][0], wq_ref[...]))
            emit(g, 1, head_norm(qk[g][1], wk_ref[...]))
            emit(g, 2, piece(g, 2))
            yield

    _interleave([sub_tile(k) for k in range(QKV_SUBTILES)])


def _qkv_proj(x, norm_w, w_qkv, wq_t, wk_t):
    bn, s, _ = x.shape
    tm = QKV_TM
    out_shapes, out_specs, scratch = [], [], []
    for d in B_DILATIONS:
        if d == 1:
            shape = jax.ShapeDtypeStruct((bn, s, B_GROUP_WIDTH), BF16)
            spec = pl.BlockSpec((1, tm, B_GROUP_WIDTH), lambda b, j: (b, j, 0))
        else:
            shape = jax.ShapeDtypeStruct((bn, d, s // d, B_GROUP_WIDTH), BF16)
            spec = pl.BlockSpec((1, d, tm // d, B_GROUP_WIDTH), lambda b, j: (b, 0, j, 0))
            scratch += [pltpu.VMEM((SLABS, tm // QKV_SUBTILES, LANES), F32)] * 3
        out_shapes += [shape] * 3
        out_specs += [spec] * 3
    return pl.pallas_call(
        functools.partial(_qkv_kernel, tm=tm),
        grid=(bn, s // tm),
        in_specs=[
            pl.BlockSpec((1, tm, D_MODEL), lambda b, j: (b, j, 0)),
            _const_spec((1, D_MODEL)),
            pl.BlockSpec((D_MODEL, OFF_BZ - OFF_BQ), lambda b, j: (0, OFF_BQ // (OFF_BZ - OFF_BQ)),
                         pipeline_mode=pl.Buffered(1)),
            _const_spec((1, B_GROUP_WIDTH)),
            _const_spec((1, B_GROUP_WIDTH)),
        ],
        out_specs=out_specs,
        out_shape=out_shapes,
        scratch_shapes=scratch * QKV_SUBTILES,
        compiler_params=pltpu.CompilerParams(
            dimension_semantics=("parallel", "parallel"), vmem_limit_bytes=VMEM_LIMIT_BYTES),
        name="qkv_proj",
    )(x, norm_w, w_qkv, wq_t, wk_t)


def _attn_kernel(q0, k0, v0, q1, k1, v1, q2, k2, v2, bias_ref, yb_ref, o_scr, lse_scr, *, s):
    lane_head = lax.broadcasted_iota(jnp.int32, (1, B_GROUP_WIDTH), 1) // B_HEAD_DIM
    masks = [lane_head == h for h in range(B_HEADS_PER_GROUP)]

    def block(g, q, k, v, n, start):
        qb = q[n * BLOCK:(n + 1) * BLOCK, :]
        if n == 0:
            kb, vb = k[0:BLOCK, :], v[0:BLOCK, :]
            bias = bias_ref[g, :, BLOCK:2 * BLOCK]
        else:
            kb, vb = k[(n - 1) * BLOCK:(n + 1) * BLOCK, :], v[(n - 1) * BLOCK:(n + 1) * BLOCK, :]
            bias = bias_ref[g]
        qs = jnp.concatenate([jnp.where(m, qb, jnp.zeros_like(qb)) for m in masks], axis=0)
        sc = _dot_nt(qs, kb) + bias
        yield
        mx = jnp.max(sc, axis=-1, keepdims=True)
        p = jnp.exp2(sc - mx)
        l = jnp.sum(p, axis=-1, keepdims=True)
        yield
        pv = _dot(p.astype(BF16), vb)
        zero = jnp.zeros((BLOCK, B_GROUP_WIDTH), F32)
        pv_sel, l_sel, mx_sel = zero, zero, zero
        for h, m in enumerate(masks):
            rows = slice(h * BLOCK, (h + 1) * BLOCK)
            pv_sel = jnp.where(m, pv[rows], pv_sel)
            l_sel = jnp.where(m, l[rows], l_sel)
            mx_sel = jnp.where(m, mx[rows], mx_sel)
        o_out = pv_sel / l_sel
        lse_out = mx_sel + jnp.log2(l_sel)
        d = B_DILATIONS[g]
        idx = pl.ds(start, BLOCK) if d == 1 else pl.ds(start, BLOCK, stride=d)
        for j in range(SLABS):
            o_scr[g, j, idx, :] = o_out[:, j * LANES:(j + 1) * LANES]
            lse_scr[g, j, idx, :] = lse_out[:, j * LANES:(j + 1) * LANES]

    blocks = []
    for g, (q, k, v) in enumerate(((q0, k0, v0), (q1, k1, v1), (q2, k2, v2))):
        d = B_DILATIONS[g]
        for res in range(d):
            qr, kr, vr = (t.at[0] if d == 1 else t.at[0, res] for t in (q, k, v))
            blocks += [block(g, qr, kr, vr, n, res + n * (BLOCK * d)) for n in range(s // d // BLOCK)]
    _interleave(blocks)

    def merge_body(c, carry):
        rows = pl.ds(pl.multiple_of(c * MERGE_ROWS, MERGE_ROWS), MERGE_ROWS)
        for j in range(SLABS):
            l0, l1, l2 = lse_scr[0, j, rows, :], lse_scr[1, j, rows, :], lse_scr[2, j, rows, :]
            lm = jnp.maximum(jnp.maximum(l0, l1), l2)
            e0, e1, e2 = jnp.exp2(l0 - lm), jnp.exp2(l1 - lm), jnp.exp2(l2 - lm)
            y = (e0 * o_scr[0, j, rows, :] + e1 * o_scr[1, j, rows, :] + e2 * o_scr[2, j, rows, :]) / (e0 + e1 + e2)
            yb_ref[0, rows, j * LANES:(j + 1) * LANES] = y.astype(yb_ref.dtype)
        return carry

    lax.fori_loop(0, s // MERGE_ROWS, merge_body, 0)


def _dilated_attention(qkv, bias):
    bn, s, _ = qkv[0].shape
    in_specs = []
    for d in B_DILATIONS:
        if d == 1:
            spec = pl.BlockSpec((1, s, B_GROUP_WIDTH), lambda b: (b, 0, 0))
        else:
            spec = pl.BlockSpec((1, d, s // d, B_GROUP_WIDTH), lambda b: (b, 0, 0, 0))
        in_specs += [spec] * 3
    in_specs.append(_const_spec((B_GROUPS, B_HEADS_PER_GROUP * BLOCK, 2 * BLOCK)))
    return pl.pallas_call(
        functools.partial(_attn_kernel, s=s),
        grid=(bn,),
        in_specs=in_specs,
        out_specs=pl.BlockSpec((1, s, B_GROUP_WIDTH), lambda b: (b, 0, 0)),
        out_shape=jax.ShapeDtypeStruct((bn, s, B_GROUP_WIDTH), BF16),
        scratch_shapes=[pltpu.VMEM((B_GROUPS, SLABS, s, LANES), F32),
                        pltpu.VMEM((B_GROUPS, SLABS, s, LANES), F32)],
        compiler_params=pltpu.CompilerParams(
            dimension_semantics=("parallel",), vmem_limit_bytes=VMEM_LIMIT_BYTES),
        name="dilated_attn",
    )(*qkv, bias)


def _memkv_kernel(mem_ref, nw_ref, w_ref, kw_ref, mk_ref, mv_ref):
    rows = mem_ref.shape[0] * MEM_LEN
    hm = (_rms_rows(mem_ref[...].reshape(rows, D_MODEL)) * nw_ref[...]).astype(BF16)
    kv = _dot(hm, w_ref[...])
    for h in range(M_HEADS):
        cols = slice(h * M_HEAD_DIM, (h + 1) * M_HEAD_DIM)
        mk = (_rms_rows(kv[:, cols]) * kw_ref[...]).astype(BF16)
        mk_ref[:, :, cols] = mk.reshape(mem_ref.shape[0], MEM_LEN, M_HEAD_DIM)
    mv_ref[...] = kv[:, M_WIDTH:].astype(BF16).reshape(mv_ref.shape)


def _mem_kv(mem, mem_norm_w, w_kv, k_norm_w):
    bn = mem.shape[0]
    mb = math.gcd(bn, MEMKV_BATCH)
    out = jax.ShapeDtypeStruct((bn, MEM_LEN, M_WIDTH), BF16)
    ospec = pl.BlockSpec((mb, MEM_LEN, M_WIDTH), lambda b: (b, 0, 0))
    return pl.pallas_call(
        _memkv_kernel,
        grid=(bn // mb,),
        in_specs=[
            pl.BlockSpec((mb, MEM_LEN, D_MODEL), lambda b: (b, 0, 0)),
            _const_spec((1, D_MODEL)),
            _const_spec((D_MODEL, 2 * M_WIDTH)),
            _const_spec((1, M_HEAD_DIM)),
        ],
        out_specs=[ospec, ospec],
        out_shape=[out, out],
        compiler_params=pltpu.CompilerParams(dimension_semantics=("parallel",)),
        name="mem_kv",
    )(mem, mem_norm_w, w_kv, k_norm_w)


MW_AU, MW_AV, MW_AZ = 0, A_WIDTH, 2 * A_WIDTH
MW_BZ = OFF_BZ
MW_MQ = MW_BZ + B_GROUP_WIDTH
MW_MZ = MW_MQ + M_WIDTH
MW_G = MW_MZ + M_WIDTH
MW_TOTAL = MW_G + 3 * D_MODEL
MW_HI = MW_TOTAL // 2
assert MW_TOTAL % 2 == 0 and MW_HI % LANES == 0 and OFF_BQ <= MW_HI <= OFF_BZ


def _main_kernel(x_ref, yb_ref, mk_ref, mv_ref,
                 nw_ref, w_ref, wh_ref, gb_ref, lnw_ref, lnb_ref, sw_ref, sb_ref, mqw_ref,
                 pa_ref, pb_ref, pm_ref, wo_ref, out_ref, *, tm):
    ti = lax.broadcasted_iota(jnp.int32, (CHUNK, CHUNK), 0)
    si = lax.broadcasted_iota(jnp.int32, (CHUNK, CHUNK), 1)
    ws = [jnp.where(si <= ti, sw_ref[g], 0.0).astype(BF16) for g in range(A_GROUPS)]
    win_lo = [(g * A_GROUP_DIM) // LANES * LANES for g in range(A_GROUPS)]
    lane = lax.broadcasted_iota(jnp.int32, (1, LANES), 1)
    in_first = lane < (A_GROUP_DIM - LANES)

    def sub_tile(rows):
        n_rows = rows.stop - rows.start
        xf = x_ref[rows, :]
        h = (_rms_rows(xf) * nw_ref[...]).astype(BF16)

        def proj(lo, width):
            ref, base = (w_ref, 0) if lo < OFF_BQ else (wh_ref, MW_HI)
            return _dot(h, ref[:, lo - base:lo - base + width])

        def gate(i):
            return jax.nn.sigmoid(proj(MW_G + i * D_MODEL, D_MODEL) + gb_ref[i:i + 1, :])

        def mem_head(mq, hd):
            cols = slice(hd * M_HEAD_DIM, (hd + 1) * M_HEAD_DIM)
            qn = (_rms_rows(mq[:, cols]) * mqw_ref[...]).astype(BF16)
            sc = _dot_nt(qn, mk_ref[0, :, cols]) * (M_HEAD_DIM ** -0.5)
            sc = sc - jnp.max(sc, axis=-1, keepdims=True)
            p = jnp.exp(sc)
            l = jnp.sum(p, axis=-1, keepdims=True)
            return _dot(p.astype(BF16), mv_ref[0, :, cols]) / l

        a_u = proj(MW_AU, A_WIDTH)
        a_v = proj(MW_AV, A_WIDTH)
        yield
        u = _gelu_exact(a_u)
        gv = _gelu_exact(a_v)
        yield
        mq = proj(MW_MQ, M_WIDTH)
        a_z = proj(MW_AZ, A_WIDTH)
        yield
        mu = jnp.mean(gv, axis=-1, keepdims=True)
        gc = gv - mu
        vv = gc * lax.rsqrt(jnp.mean(gc * gc, axis=-1, keepdims=True) + EPS) * lnw_ref[...] + lnb_ref[...]
        vv = vv.astype(BF16)
        mixed_chunks = []
        for c in range(n_rows // CHUNK):
            vc = vv[c * CHUNK:(c + 1) * CHUNK, :]
            m = [_dot(ws[g], vc[:, win_lo[g]:win_lo[g] + 2 * LANES]) for g in range(A_GROUPS)]
            lo, hi = (lambda t: t[:, :LANES]), (lambda t: t[:, LANES:])
            pieces = [lo(m[0]), jnp.where(in_first, hi(m[0]), lo(m[1])), hi(m[1]),
                      lo(m[2]), jnp.where(in_first, hi(m[2]), lo(m[3])), hi(m[3])]
            mixed_chunks.append(jnp.concatenate(pieces, axis=-1) + sb_ref[...])
        mixed = jnp.concatenate(mixed_chunks, axis=0)
        y_a = (u * mixed * jax.nn.silu(a_z)).astype(BF16)
        yield
        acc = gate(0) * _dot(y_a, pa_ref[...])
        yield

        ym_heads = [mem_head(mq, hd) for hd in range(M_HEADS)]
        yield
        m_z = proj(MW_MZ, M_WIDTH)
        b_z = proj(MW_BZ, B_GROUP_WIDTH)
        yield
        y_m = (jnp.concatenate(ym_heads, axis=-1) * jax.nn.silu(m_z)).astype(BF16)
        y_b = (yb_ref[rows, :].astype(F32) * jax.nn.silu(b_z)).astype(BF16)
        yield
        acc = acc + gate(2) * _dot(y_m, pm_ref[...])
        yield
        acc = acc + gate(1) * _dot(y_b, pb_ref[...])
        yield
        out_ref[rows, :] = xf + _dot(acc.astype(BF16), wo_ref[...])

    rows_per = tm // MAIN_SUBTILES
    _interleave([sub_tile(slice(k * rows_per, (k + 1) * rows_per)) for k in range(MAIN_SUBTILES)])


def _main_block(x2, yb, mk, mv, norm_w, w_main, gate_b, lnw, lnb, sw, sb_full, mqw,
                proj_a, proj_b, proj_m, w_out, s):
    n = x2.shape[0]
    tm = MAIN_TM
    steps_per_batch = s // tm
    tile = lambda width: pl.BlockSpec((tm, width), lambda i: (i, 0))
    mem_spec = pl.BlockSpec((1, MEM_LEN, M_WIDTH), lambda i: (i // steps_per_batch, 0, 0))
    return pl.pallas_call(
        functools.partial(_main_kernel, tm=tm),
        grid=(n // tm,),
        in_specs=[tile(D_MODEL), tile(B_GROUP_WIDTH), mem_spec, mem_spec,
                  _const_spec((1, D_MODEL)),
                  pl.BlockSpec((D_MODEL, OFF_BQ), lambda i: (0, 0), pipeline_mode=pl.Buffered(1)),
                  pl.BlockSpec((D_MODEL, MW_HI), lambda i: (0, 1), pipeline_mode=pl.Buffered(1)),
                  _const_spec((3, D_MODEL)),
                  _const_spec((1, A_WIDTH)),
                  _const_spec((1, A_WIDTH)),
                  _const_spec((A_GROUPS, CHUNK, CHUNK)),
                  _const_spec((CHUNK, A_WIDTH)),
                  _const_spec((1, M_HEAD_DIM)),
                  _const_spec((A_WIDTH, D_MODEL)),
                  _const_spec((B_GROUP_WIDTH, D_MODEL)),
                  _const_spec((M_WIDTH, D_MODEL)),
                  _const_spec((D_MODEL, D_MODEL))],
        out_specs=tile(D_MODEL),
        out_shape=jax.ShapeDtypeStruct((n, D_MODEL), F32),
        compiler_params=pltpu.CompilerParams(
            dimension_semantics=("parallel",), vmem_limit_bytes=VMEM_LIMIT_BYTES),
        name="main_block",
    )(x2, yb, mk, mv, norm_w, w_main, w_main, gate_b, lnw, lnb, sw, sb_full, mqw,
      proj_a, proj_b, proj_m, w_out)


def kernel(x, mem, norm_w, w_in, gate_b, a_v_norm_w, a_v_norm_b, a_spatial_w, a_spatial_b,
           b_q_norm_w, b_k_norm_w, rel_bias, mem_norm_w, m_w_kv, m_q_norm_w, m_k_norm_w,
           proj_a, proj_b, proj_m, w_out):
    bn, s, _ = x.shape
    n = bn * s
    row = lambda v: v.reshape(1, -1)

    w_bf = w_in.astype(BF16)
    wq_t = row(jnp.tile(b_q_norm_w * (B_HEAD_DIM ** -0.5 * LOG2E), B_HEADS_PER_GROUP))
    wk_t = row(jnp.tile(b_k_norm_w, B_HEADS_PER_GROUP))
    sb_full = jnp.repeat(a_spatial_b.T, A_GROUP_DIM, axis=1)

    bias = _expand_bias(rel_bias)
    qkv = _qkv_proj(x, row(norm_w), w_bf, wq_t, wk_t)
    yb = _dilated_attention(qkv, bias)
    mk, mv = _mem_kv(mem, row(mem_norm_w), m_w_kv.astype(BF16), row(m_k_norm_w))
    out = _main_block(x.reshape(n, D_MODEL), yb.reshape(n, B_GROUP_WIDTH), mk, mv, row(norm_w), w_bf, gate_b,
                      row(a_v_norm_w), row(a_v_norm_b), a_spatial_w, sb_full, row(m_q_norm_w),
                      proj_a.astype(BF16), proj_b.astype(BF16), proj_m.astype(BF16), w_out.astype(BF16), s)
    return out.reshape(bn, s, D_MODEL)
```

```python
import functools
import math

import jax
import jax.numpy as jnp
import numpy as np
from jax import lax
from jax.experimental import pallas as pl
from jax.experimental.pallas import tpu as pltpu

EPS = 1e-6
D_MODEL = 1024
A_WIDTH = 768
A_GROUPS = 4
A_GROUP_DIM = A_WIDTH // A_GROUPS
CHUNK = 128
B_PATTERNS = ((128, 1), (512, 4), (2048, 16))
B_DILATIONS = tuple(d for _, d in B_PATTERNS)
B_GROUPS = 3
B_HEADS_PER_GROUP = 4
B_HEAD_DIM = 64
B_GROUP_WIDTH = B_HEADS_PER_GROUP * B_HEAD_DIM
B_QKV_WIDTH = B_GROUPS * B_GROUP_WIDTH
BLOCK = 128
MEM_LEN = 256
M_HEADS = 4
M_HEAD_DIM = 128
M_WIDTH = M_HEADS * M_HEAD_DIM
REL_BUCKETS = 32
REL_MAX_DISTANCE = 2048
NEG = -1e30
LOG2E = math.log2(math.e)

OFF_BQ = 3 * A_WIDTH
OFF_BZ = OFF_BQ + 3 * B_QKV_WIDTH

LANES = 128
SLABS = B_GROUP_WIDTH // LANES
VMEM_LIMIT_BYTES = 60 * 1024 * 1024

QKV_TM = 1024
MAIN_TM = 1024
MERGE_ROWS = 256
MEMKV_BATCH = 4
MAIN_SUBTILES = 2
QKV_SUBTILES = 2
assert OFF_BQ % (OFF_BZ - OFF_BQ) == 0

BF16 = jnp.bfloat16
F32 = jnp.float32


def _dot(a, b):
    return jnp.dot(a, b, preferred_element_type=F32)


def _dot_nt(a, b):
    return lax.dot_general(a, b, (((1,), (1,)), ((), ())), preferred_element_type=F32)


def _rms_rows(xf):
    return xf * lax.rsqrt(jnp.mean(xf * xf, axis=-1, keepdims=True) + EPS)


def _gelu_exact(t):
    return 0.5 * t * (1.0 + lax.erf(t * (2.0 ** -0.5)))


def _interleave(stage_generators):
    done = object()
    queue, live = list(stage_generators), []
    while queue or live:
        if queue:
            live.append(queue.pop(0))
        live = [gen for gen in live if next(gen, done) is not done]


def _const_spec(shape):
    nd = len(shape)
    return pl.BlockSpec(shape, lambda *_: (0,) * nd, pipeline_mode=pl.Buffered(1))


def _t5_causal_bucket(dist):
    max_exact = REL_BUCKETS // 2
    df = np.maximum(dist, 1).astype(np.float32)
    scaled = (np.log(df / np.float32(max_exact)) / np.float32(math.log(REL_MAX_DISTANCE / max_exact))
              * np.float32(REL_BUCKETS - max_exact))
    large = np.minimum(max_exact + scaled.astype(np.int32), REL_BUCKETS - 1)
    return np.where(dist < max_exact, dist, large).astype(np.int32)


def _bias_kernel(bucket_ref, valid_ref, rb_ref, out_ref):
    g = pl.program_id(0)
    bk = bucket_ref[0]
    valid = valid_ref[...] != 0
    for h in range(B_HEADS_PER_GROUP):
        acc = jnp.zeros((BLOCK, 2 * BLOCK), F32)
        for b in range(REL_BUCKETS):
            acc = jnp.where(bk == b, rb_ref[b, g * B_HEADS_PER_GROUP + h], acc)
        out_ref[0, h * BLOCK:(h + 1) * BLOCK, :] = jnp.where(valid, acc * LOG2E, NEG)


def _expand_bias(rel_bias):
    qi = np.arange(BLOCK, dtype=np.int32)[:, None]
    kj = np.arange(2 * BLOCK, dtype=np.int32)[None, :]
    step = qi + BLOCK - kj
    buckets = np.stack([_t5_causal_bucket(np.maximum(step, 0) * d) for d in B_DILATIONS])
    valid = ((step >= 0) & (step <= BLOCK)).astype(np.int32)
    return pl.pallas_call(
        _bias_kernel,
        grid=(B_GROUPS,),
        in_specs=[
            pl.BlockSpec((1, BLOCK, 2 * BLOCK), lambda g: (g, 0, 0)),
            pl.BlockSpec((BLOCK, 2 * BLOCK), lambda g: (0, 0)),
            pl.BlockSpec(memory_space=pltpu.SMEM),
        ],
        out_specs=pl.BlockSpec((1, B_HEADS_PER_GROUP * BLOCK, 2 * BLOCK), lambda g: (g, 0, 0)),
        out_shape=jax.ShapeDtypeStruct((B_GROUPS, B_HEADS_PER_GROUP * BLOCK, 2 * BLOCK), F32),
        name="rel_bias_expand",
    )(buckets, valid, rel_bias)


def _qkv_kernel(x_ref, nw_ref, w_ref, wq_ref, wk_ref, *refs, tm):
    out_refs = refs[:3 * B_GROUPS]
    scratch = refs[3 * B_GROUPS:]
    n_slabs = 3 * (B_GROUPS - 1)
    rows = tm // QKV_SUBTILES
    r = lax.broadcasted_iota(jnp.int32, (B_GROUP_WIDTH, B_GROUP_WIDTH), 0) // B_HEAD_DIM
    c = lax.broadcasted_iota(jnp.int32, (B_GROUP_WIDTH, B_GROUP_WIDTH), 1) // B_HEAD_DIM
    seg = jnp.where(r == c, 1.0 / B_HEAD_DIM, 0.0).astype(BF16)

    def head_norm(t, w):
        ms = _dot((t * t).astype(BF16), seg)
        return t * lax.rsqrt(ms + EPS) * w

    def sub_tile(k):
        h = (_rms_rows(x_ref[0, k * rows:(k + 1) * rows, :]) * nw_ref[...]).astype(BF16)

        def emit(g, which, t):
            out = out_refs[3 * g + which]
            d = B_DILATIONS[g]
            sub = rows // d
            if d == 1:
                out[0, k * rows:(k + 1) * rows, :] = t.astype(BF16)
                return
            scr = scratch[k * n_slabs + 3 * (g - 1) + which]
            for j in range(SLABS):
                scr[j] = t[:, j * LANES:(j + 1) * LANES]
            for res in range(d):
                for j in range(SLABS):
                    out[0, res, k * sub:(k + 1) * sub, j * LANES:(j + 1) * LANES] = (
                        scr[j, pl.ds(res, sub, stride=d), :].astype(BF16))

        def piece(g, which):
            lo = which * B_QKV_WIDTH + g * B_GROUP_WIDTH
            return _dot(h, w_ref[:, lo:lo + B_GROUP_WIDTH])

        qk = [[piece(g, which) for which in range(2)] for g in range(B_GROUPS)]
        yield
        for g in reversed(range(B_GROUPS)):
            emit(g, 0, head_norm(qk[g][0], wq_ref[...]))
            emit(g, 1, head_norm(qk[g][1], wk_ref[...]))
            emit(g, 2, piece(g, 2))
            yield

    _interleave([sub_tile(k) for k in range(QKV_SUBTILES)])


def _qkv_proj(x, norm_w, w_qkv, wq_t, wk_t):
    bn, s, _ = x.shape
    tm = QKV_TM
    out_shapes, out_specs, scratch = [], [], []
    for d in B_DILATIONS:
        if d == 1:
            shape = jax.ShapeDtypeStruct((bn, s, B_GROUP_WIDTH), BF16)
            spec = pl.BlockSpec((1, tm, B_GROUP_WIDTH), lambda b, j: (b, j, 0))
        else:
            shape = jax.ShapeDtypeStruct((bn, d, s // d, B_GROUP_WIDTH), BF16)
            spec = pl.BlockSpec((1, d, tm // d, B_GROUP_WIDTH), lambda b, j: (b, 0, j, 0))
            scratch += [pltpu.VMEM((SLABS, tm // QKV_SUBTILES, LANES), F32)] * 3
        out_shapes += [shape] * 3
        out_specs += [spec] * 3
    return pl.pallas_call(
        functools.partial(_qkv_kernel, tm=tm),
        grid=(bn, s // tm),
        in_specs=[
            pl.BlockSpec((1, tm, D_MODEL), lambda b, j: (b, j, 0)),
            _const_spec((1, D_MODEL)),
            pl.BlockSpec((D_MODEL, OFF_BZ - OFF_BQ), lambda b, j: (0, OFF_BQ // (OFF_BZ - OFF_BQ)),
                         pipeline_mode=pl.Buffered(1)),
            _const_spec((1, B_GROUP_WIDTH)),
            _const_spec((1, B_GROUP_WIDTH)),
        ],
        out_specs=out_specs,
        out_shape=out_shapes,
        scratch_shapes=scratch * QKV_SUBTILES,
        compiler_params=pltpu.CompilerParams(
            dimension_semantics=("parallel", "parallel"), vmem_limit_bytes=VMEM_LIMIT_BYTES),
        name="qkv_proj",
    )(x, norm_w, w_qkv, wq_t, wk_t)


def _attn_kernel(q0, k0, v0, q1, k1, v1, q2, k2, v2, bias_ref, yb_ref, o_scr, lse_scr, *, s):
    lane_head = lax.broadcasted_iota(jnp.int32, (1, B_GROUP_WIDTH), 1) // B_HEAD_DIM
    masks = [lane_head == h for h in range(B_HEADS_PER_GROUP)]

    def block(g, q, k, v, n, start):
        qb = q[n * BLOCK:(n + 1) * BLOCK, :]
        if n == 0:
            kb, vb = k[0:BLOCK, :], v[0:BLOCK, :]
            bias = bias_ref[g, :, BLOCK:2 * BLOCK]
        else:
            kb, vb = k[(n - 1) * BLOCK:(n + 1) * BLOCK, :], v[(n - 1) * BLOCK:(n + 1) * BLOCK, :]
            bias = bias_ref[g]
        qs = jnp.concatenate([jnp.where(m, qb, jnp.zeros_like(qb)) for m in masks], axis=0)
        sc = _dot_nt(qs, kb) + bias
        mx = jnp.max(sc, axis=-1, keepdims=True)
        p = jnp.exp2(sc - mx)
        l = jnp.sum(p, axis=-1, keepdims=True)
        yield
        pv = _dot(p.astype(BF16), vb)
        zero = jnp.zeros((BLOCK, B_GROUP_WIDTH), F32)
        pv_sel, l_sel, mx_sel = zero, zero, zero
        for h, m in enumerate(masks):
            rows = slice(h * BLOCK, (h + 1) * BLOCK)
            pv_sel = jnp.where(m, pv[rows], pv_sel)
            l_sel = jnp.where(m, l[rows], l_sel)
            mx_sel = jnp.where(m, mx[rows], mx_sel)
        o_out = pv_sel / l_sel
        lse_out = mx_sel + jnp.log2(l_sel)
        d = B_DILATIONS[g]
        idx = pl.ds(start, BLOCK) if d == 1 else pl.ds(start, BLOCK, stride=d)
        for j in range(SLABS):
            o_scr[g, j, idx, :] = o_out[:, j * LANES:(j + 1) * LANES]
            lse_scr[g, j, idx, :] = lse_out[:, j * LANES:(j + 1) * LANES]

    blocks = []
    for g, (q, k, v) in enumerate(((q0, k0, v0), (q1, k1, v1), (q2, k2, v2))):
        d = B_DILATIONS[g]
        for res in range(d):
            qr, kr, vr = (t.at[0] if d == 1 else t.at[0, res] for t in (q, k, v))
            blocks += [block(g, qr, kr, vr, n, res + n * (BLOCK * d)) for n in range(s // d // BLOCK)]
    _interleave(blocks)

    def merge_body(c, carry):
        rows = pl.ds(pl.multiple_of(c * MERGE_ROWS, MERGE_ROWS), MERGE_ROWS)
        for j in range(SLABS):
            l0, l1, l2 = lse_scr[0, j, rows, :], lse_scr[1, j, rows, :], lse_scr[2, j, rows, :]
            lm = jnp.maximum(jnp.maximum(l0, l1), l2)
            e0, e1, e2 = jnp.exp2(l0 - lm), jnp.exp2(l1 - lm), jnp.exp2(l2 - lm)
            y = (e0 * o_scr[0, j, rows, :] + e1 * o_scr[1, j, rows, :] + e2 * o_scr[2, j, rows, :]) / (e0 + e1 + e2)
            yb_ref[0, rows, j * LANES:(j + 1) * LANES] = y.astype(yb_ref.dtype)
        return carry

    lax.fori_loop(0, s // MERGE_ROWS, merge_body, 0)


def _dilated_attention(qkv, bias):
    bn, s, _ = qkv[0].shape
    in_specs = []
    for d in B_DILATIONS:
        if d == 1:
            spec = pl.BlockSpec((1, s, B_GROUP_WIDTH), lambda b: (b, 0, 0))
        else:
            spec = pl.BlockSpec((1, d, s // d, B_GROUP_WIDTH), lambda b: (b, 0, 0, 0))
        in_specs += [spec] * 3
    in_specs.append(_const_spec((B_GROUPS, B_HEADS_PER_GROUP * BLOCK, 2 * BLOCK)))
    return pl.pallas_call(
        functools.partial(_attn_kernel, s=s),
        grid=(bn,),
        in_specs=in_specs,
        out_specs=pl.BlockSpec((1, s, B_GROUP_WIDTH), lambda b: (b, 0, 0)),
        out_shape=jax.ShapeDtypeStruct((bn, s, B_GROUP_WIDTH), BF16),
        scratch_shapes=[pltpu.VMEM((B_GROUPS, SLABS, s, LANES), F32),
                        pltpu.VMEM((B_GROUPS, SLABS, s, LANES), F32)],
        compiler_params=pltpu.CompilerParams(
            dimension_semantics=("parallel",), vmem_limit_bytes=VMEM_LIMIT_BYTES),
        name="dilated_attn",
    )(*qkv, bias)


def _memkv_kernel(mem_ref, nw_ref, w_ref, kw_ref, mk_ref, mv_ref):
    rows = mem_ref.shape[0] * MEM_LEN
    hm = (_rms_rows(mem_ref[...].reshape(rows, D_MODEL)) * nw_ref[...]).astype(BF16)
    kv = _dot(hm, w_ref[...])
    for h in range(M_HEADS):
        cols = slice(h * M_HEAD_DIM, (h + 1) * M_HEAD_DIM)
        mk = (_rms_rows(kv[:, cols]) * kw_ref[...]).astype(BF16)
        mk_ref[:, :, cols] = mk.reshape(mem_ref.shape[0], MEM_LEN, M_HEAD_DIM)
    mv_ref[...] = kv[:, M_WIDTH:].astype(BF16).reshape(mv_ref.shape)


def _mem_kv(mem, mem_norm_w, w_kv, k_norm_w):
    bn = mem.shape[0]
    mb = math.gcd(bn, MEMKV_BATCH)
    out = jax.ShapeDtypeStruct((bn, MEM_LEN, M_WIDTH), BF16)
    ospec = pl.BlockSpec((mb, MEM_LEN, M_WIDTH), lambda b: (b, 0, 0))
    return pl.pallas_call(
        _memkv_kernel,
        grid=(bn // mb,),
        in_specs=[
            pl.BlockSpec((mb, MEM_LEN, D_MODEL), lambda b: (b, 0, 0)),
            _const_spec((1, D_MODEL)),
            _const_spec((D_MODEL, 2 * M_WIDTH)),
            _const_spec((1, M_HEAD_DIM)),
        ],
        out_specs=[ospec, ospec],
        out_shape=[out, out],
        compiler_params=pltpu.CompilerParams(dimension_semantics=("parallel",)),
        name="mem_kv",
    )(mem, mem_norm_w, w_kv, k_norm_w)


MW_AU, MW_AV, MW_AZ = 0, A_WIDTH, 2 * A_WIDTH
MW_BZ = OFF_BZ
MW_MQ = MW_BZ + B_GROUP_WIDTH
MW_MZ = MW_MQ + M_WIDTH
MW_G = MW_MZ + M_WIDTH
MW_TOTAL = MW_G + 3 * D_MODEL
MW_HI = MW_TOTAL // 2
assert MW_TOTAL % 2 == 0 and MW_HI % LANES == 0 and OFF_BQ <= MW_HI <= OFF_BZ


def _main_kernel(x_ref, yb_ref, mk_ref, mv_ref,
                 nw_ref, w_ref, wh_ref, gb_ref, lnw_ref, lnb_ref, sw_ref, sb_ref, mqw_ref,
                 pa_ref, pb_ref, pm_ref, wo_ref, out_ref, *, tm):
    ti = lax.broadcasted_iota(jnp.int32, (CHUNK, CHUNK), 0)
    si = lax.broadcasted_iota(jnp.int32, (CHUNK, CHUNK), 1)
    ws = [jnp.where(si <= ti, sw_ref[g], 0.0).astype(BF16) for g in range(A_GROUPS)]
    win_lo = [(g * A_GROUP_DIM) // LANES * LANES for g in range(A_GROUPS)]
    lane = lax.broadcasted_iota(jnp.int32, (1, LANES), 1)
    in_first = lane < (A_GROUP_DIM - LANES)

    def sub_tile(rows):
        n_rows = rows.stop - rows.start
        xf = x_ref[rows, :]
        h = (_rms_rows(xf) * nw_ref[...]).astype(BF16)

        def proj(lo, width):
            ref, base = (w_ref, 0) if lo < OFF_BQ else (wh_ref, MW_HI)
            return _dot(h, ref[:, lo - base:lo - base + width])

        def gate(i):
            return jax.nn.sigmoid(proj(MW_G + i * D_MODEL, D_MODEL) + gb_ref[i:i + 1, :])

        def mem_head(mq, hd):
            cols = slice(hd * M_HEAD_DIM, (hd + 1) * M_HEAD_DIM)
            qn = (_rms_rows(mq[:, cols]) * mqw_ref[...]).astype(BF16)
            sc = _dot_nt(qn, mk_ref[0, :, cols]) * (M_HEAD_DIM ** -0.5)
            sc = sc - jnp.max(sc, axis=-1, keepdims=True)
            p = jnp.exp(sc)
            l = jnp.sum(p, axis=-1, keepdims=True)
            return _dot(p.astype(BF16), mv_ref[0, :, cols]) / l

        a_u = proj(MW_AU, A_WIDTH)
        a_v = proj(MW_AV, A_WIDTH)
        yield
        u = _gelu_exact(a_u)
        gv = _gelu_exact(a_v)
        yield
        mq = proj(MW_MQ, M_WIDTH)
        a_z = proj(MW_AZ, A_WIDTH)
        yield
        mu = jnp.mean(gv, axis=-1, keepdims=True)
        gc = gv - mu
        vv = gc * lax.rsqrt(jnp.mean(gc * gc, axis=-1, keepdims=True) + EPS) * lnw_ref[...] + lnb_ref[...]
        vv = vv.astype(BF16)
        mixed_chunks = []
        for c in range(n_rows // CHUNK):
            vc = vv[c * CHUNK:(c + 1) * CHUNK, :]
            m = [_dot(ws[g], vc[:, win_lo[g]:win_lo[g] + 2 * LANES]) for g in range(A_GROUPS)]
            lo, hi = (lambda t: t[:, :LANES]), (lambda t: t[:, LANES:])
            pieces = [lo(m[0]), jnp.where(in_first, hi(m[0]), lo(m[1])), hi(m[1]),
                      lo(m[2]), jnp.where(in_first, hi(m[2]), lo(m[3])), hi(m[3])]
            mixed_chunks.append(jnp.concatenate(pieces, axis=-1) + sb_ref[...])
        mixed = jnp.concatenate(mixed_chunks, axis=0)
        y_a = (u * mixed * jax.nn.silu(a_z)).astype(BF16)
        yield
        acc = gate(0) * _dot(y_a, pa_ref[...])
        yield

        ym_heads = [mem_head(mq, hd) for hd in range(M_HEADS)]
        yield
        m_z = proj(MW_MZ, M_WIDTH)
        b_z = proj(MW_BZ, B_GROUP_WIDTH)
        yield
        y_m = (jnp.concatenate(ym_heads, axis=-1) * jax.nn.silu(m_z)).astype(BF16)
        y_b = (yb_ref[rows, :].astype(F32) * jax.nn.silu(b_z)).astype(BF16)
        yield
        acc = acc + gate(2) * _dot(y_m, pm_ref[...])
        yield
        acc = acc + gate(1) * _dot(y_b, pb_ref[...])
        yield
        out_ref[rows, :] = xf + _dot(acc.astype(BF16), wo_ref[...])

    rows_per = tm // MAIN_SUBTILES
    _interleave([sub_tile(slice(k * rows_per, (k + 1) * rows_per)) for k in range(MAIN_SUBTILES)])


def _main_block(x2, yb, mk, mv, norm_w, w_main, gate_b, lnw, lnb, sw, sb_full, mqw,
                proj_a, proj_b, proj_m, w_out, s):
    n = x2.shape[0]
    tm = MAIN_TM
    steps_per_batch = s // tm
    tile = lambda width: pl.BlockSpec((tm, width), lambda i: (i, 0))
    mem_spec = pl.BlockSpec((1, MEM_LEN, M_WIDTH), lambda i: (i // steps_per_batch, 0, 0))
    return pl.pallas_call(
        functools.partial(_main_kernel, tm=tm),
        grid=(n // tm,),
        in_specs=[tile(D_MODEL), tile(B_GROUP_WIDTH), mem_spec, mem_spec,
                  _const_spec((1, D_MODEL)),
                  pl.BlockSpec((D_MODEL, OFF_BQ), lambda i: (0, 0), pipeline_mode=pl.Buffered(1)),
                  pl.BlockSpec((D_MODEL, MW_HI), lambda i: (0, 1), pipeline_mode=pl.Buffered(1)),
                  _const_spec((3, D_MODEL)),
                  _const_spec((1, A_WIDTH)),
                  _const_spec((1, A_WIDTH)),
                  _const_spec((A_GROUPS, CHUNK, CHUNK)),
                  _const_spec((CHUNK, A_WIDTH)),
                  _const_spec((1, M_HEAD_DIM)),
                  _const_spec((A_WIDTH, D_MODEL)),
                  _const_spec((B_GROUP_WIDTH, D_MODEL)),
                  _const_spec((M_WIDTH, D_MODEL)),
                  _const_spec((D_MODEL, D_MODEL))],
        out_specs=tile(D_MODEL),
        out_shape=jax.ShapeDtypeStruct((n, D_MODEL), F32),
        compiler_params=pltpu.CompilerParams(
            dimension_semantics=("parallel",), vmem_limit_bytes=VMEM_LIMIT_BYTES),
        name="main_block",
    )(x2, yb, mk, mv, norm_w, w_main, w_main, gate_b, lnw, lnb, sw, sb_full, mqw,
      proj_a, proj_b, proj_m, w_out)


def kernel(x, mem, norm_w, w_in, gate_b, a_v_norm_w, a_v_norm_b, a_spatial_w, a_spatial_b,
           b_q_norm_w, b_k_norm_w, rel_bias, mem_norm_w, m_w_kv, m_q_norm_w, m_k_norm_w,
           proj_a, proj_b, proj_m, w_out):
    bn, s, _ = x.shape
    n = bn * s
    row = lambda v: v.reshape(1, -1)

    w_bf = w_in.astype(BF16)
    wq_t = row(jnp.tile(b_q_norm_w * (B_HEAD_DIM ** -0.5 * LOG2E), B_HEADS_PER_GROUP))
    wk_t = row(jnp.tile(b_k_norm_w, B_HEADS_PER_GROUP))
    sb_full = jnp.repeat(a_spatial_b.T, A_GROUP_DIM, axis=1)

    bias = _expand_bias(rel_bias)
    qkv = _qkv_proj(x, row(norm_w), w_bf, wq_t, wk_t)
    yb = _dilated_attention(qkv, bias)
    mk, mv = _mem_kv(mem, row(mem_norm_w), m_w_kv.astype(BF16), row(m_k_norm_w))
    out = _main_block(x.reshape(n, D_MODEL), yb.reshape(n, B_GROUP_WIDTH), mk, mv, row(norm_w), w_bf, gate_b,
                      row(a_v_norm_w), row(a_v_norm_b), a_spatial_w, sb_full, row(m_q_norm_w),
                      proj_a.astype(BF16), proj_b.astype(BF16), proj_m.astype(BF16), w_out.astype(BF16), s)
    return out.reshape(bn, s, D_MODEL)
```
